```python
import jax, jax.numpy as jnp
from jax import lax
import numpy as np

D_MODEL = 2048
BATCH = 4
SEQ = 8192
DEPTH = 2

GRID_W = 64
CTX_LEN = 256
HEAD_DIM = 128
MIX_W = D_MODEL // 2
ATT_Q_HEADS = MIX_W // HEAD_DIM
ATT_KV_HEADS = ATT_Q_HEADS // 4
ATT_KV_W = ATT_KV_HEADS * HEAD_DIM
ROPE_THETA = 10000.0
Q_BLOCK = 128
SGU_GROUPS = 8
SGU_CHUNK = 128
SGU_GROUP_W = MIX_W // SGU_GROUPS
RET_HEADS = MIX_W // HEAD_DIM
RET_CHUNK = 128
N_BRANCHES = 3
N_EXPERTS = 8
TOP_K = 2
D_FF = 5632
D_FF_EXPERT = 7168
N_DENSE = (DEPTH + 1) // 2
N_MOE = DEPTH // 2
ALPHA = (2 * DEPTH) ** 0.25
BETA = (8 * DEPTH) ** -0.25
EPS = 1e-6
IN_SIZES = (MIX_W, ATT_KV_W, ATT_KV_W, MIX_W, MIX_W, MIX_W, MIX_W, MIX_W, MIX_W, N_BRANCHES * D_MODEL)
IN_COLS = sum(IN_SIZES)

kernel_name = "hybrid_gated_gqa_sgu_retention_moe_dit"


def layer_norm(x, g=None, b=None):
    xf = x.astype(jnp.float32)
    xc = xf - jnp.mean(xf, axis=-1, keepdims=True)
    y = xc * lax.rsqrt(jnp.mean(xc * xc, axis=-1, keepdims=True) + EPS)
    if g is not None:
        y = y * g.astype(jnp.float32) + b.astype(jnp.float32)
    return y.astype(x.dtype)


def rms_norm(x, g):
    xf = x.astype(jnp.float32)
    y = xf * lax.rsqrt(jnp.mean(xf * xf, axis=-1, keepdims=True) + EPS) * g.astype(jnp.float32)
    return y.astype(x.dtype)


def adaln_modulation(cond, w, b):
    m = jax.nn.silu(cond) @ w + b
    return jnp.split(m.reshape(-1, 1, 6 * D_MODEL), 6, axis=-1)


def modulate(x, shift, scale):
    return layer_norm(x) * (1.0 + scale) + shift


def split_projection(h, w_in):
    offsets = [sum(IN_SIZES[: j + 1]) for j in range(len(IN_SIZES) - 1)]
    return jnp.split(h @ w_in, offsets, axis=-1)


def to_heads(t):
    return t.reshape(t.shape[:-1] + (t.shape[-1] // HEAD_DIM, HEAD_DIM))


def axial_rope_angles(n_tokens):
    rows = n_tokens // GRID_W
    row = jnp.repeat(jnp.arange(rows, dtype=jnp.float32), GRID_W)
    col = jnp.tile(jnp.arange(GRID_W, dtype=jnp.float32), rows)
    n_freq = HEAD_DIM // 4
    inv_freq = ROPE_THETA ** (-jnp.arange(n_freq, dtype=jnp.float32) / n_freq)
    return row[:, None] * inv_freq, col[:, None] * inv_freq


def rotate_pairs(x, ang):
    n = ang.shape[-1]
    cos = jnp.cos(ang)[None, :, None, :]
    sin = jnp.sin(ang)[None, :, None, :]
    x1, x2 = x[..., :n], x[..., n:]
    return jnp.concatenate([x1 * cos - x2 * sin, x2 * cos + x1 * sin], axis=-1).astype(x.dtype)


def axial_rope(x, ang_row, ang_col):
    half = HEAD_DIM // 2
    return jnp.concatenate([rotate_pairs(x[..., :half], ang_row), rotate_pairs(x[..., half:], ang_col)], axis=-1)


def blocked_gqa(q, k, v):
    b, lq = q.shape[:2]
    n_blk = lq // Q_BLOCK
    grp = ATT_Q_HEADS // ATT_KV_HEADS
    qb = q.reshape(b, n_blk, Q_BLOCK, ATT_KV_HEADS, grp, HEAD_DIM).transpose(1, 0, 2, 3, 4, 5)
    scale = HEAD_DIM ** -0.5

    def one_block(qi):
        s = jnp.einsum('bqhgd,bkhd->bhgqk', qi, k, preferred_element_type=jnp.float32) * scale
        p = jax.nn.softmax(s, axis=-1).astype(v.dtype)
        return jnp.einsum('bhgqk,bkhd->bqhgd', p, v)

    o = lax.map(one_block, qb)
    return o.transpose(1, 0, 2, 3, 4, 5).reshape(b, lq, ATT_Q_HEADS * HEAD_DIM)


def spatial_gating(u, v, ln_g, ln_b, w_s, b_s):
    b, n = u.shape[:2]
    u = jax.nn.gelu(u)
    v = layer_norm(jax.nn.gelu(v), ln_g, ln_b)
    vc = v.reshape(b, n // SGU_CHUNK, SGU_CHUNK, SGU_GROUPS, SGU_GROUP_W)
    z = jnp.einsum('gpq,bnqgc->bnpgc', w_s, vc) + b_s.T[None, None, :, :, None]
    return u * z.reshape(b, n, MIX_W)


def retention_heads(q, k, v):
    f32 = jnp.float32
    return (to_heads(q).astype(f32), to_heads(k).astype(f32) * HEAD_DIM ** -0.5, to_heads(v).astype(f32))


def retention_scan(q, k, v, log_gamma, state0):
    b, n, h, d = q.shape
    n_chunk = n // RET_CHUNK
    pos = jnp.arange(RET_CHUNK, dtype=jnp.float32)
    diff = pos[:, None] - pos[None, :]
    dmat = jnp.where(diff >= 0, jnp.exp(jnp.maximum(diff, 0.0)[None] * log_gamma[:, None, None]), 0.0)
    xi = jnp.exp((pos[:, None] + 1.0) * log_gamma[None, :])
    zeta = jnp.exp((RET_CHUNK - 1.0 - pos)[:, None] * log_gamma[None, :])
    chunk_decay = jnp.exp(RET_CHUNK * log_gamma)

    def chunks(t):
        return t.reshape(b, n_chunk, RET_CHUNK, h, d).transpose(1, 0, 2, 3, 4)

    def step(state, qkv):
        qc, kc, vc = qkv
        inner = jnp.einsum('bnhd,bmhd->bhnm', qc, kc) * dmat[None]
        o = jnp.einsum('bhnm,bmhe->bnhe', inner, vc)
        o = o + jnp.einsum('bnhd,bhde->bnhe', qc, state) * xi[None, :, :, None]
        state = state * chunk_decay[None, :, None, None] + jnp.einsum('bmhd,bmhe->bhde', kc * zeta[None, :, :, None], vc)
        return state, o

    state, o = lax.scan(step, state0, (chunks(q), chunks(k), chunks(v)))
    return o.transpose(1, 0, 2, 3, 4).reshape(b, n, h, d), state


def retention_state(k, v, log_gamma):
    n = k.shape[1]
    w = jnp.exp((n - 1.0 - jnp.arange(n, dtype=jnp.float32))[:, None] * log_gamma[None, :])
    return jnp.einsum('blhd,blhe->bhde', k * w[None, :, :, None], v)


def bidirectional_retention(q, k, v, lg_f, lg_b, s_f0, s_b0):
    o_f, s_f = retention_scan(q, k, v, lg_f, s_f0)
    o_b, s_b = retention_scan(jnp.flip(q, 1), jnp.flip(k, 1), jnp.flip(v, 1), lg_b, s_b0)
    return o_f + jnp.flip(o_b, 1), s_f, s_b


def retention_output(o, g):
    gn = layer_norm(o).reshape(o.shape[:2] + (MIX_W,))
    return jax.nn.silu(g) * gn.astype(g.dtype)


def merge_branches(att, sgu, ret, gate_logits, w_att, w_sgu, w_ret, w_o):
    g_att, g_sgu, g_ret = jnp.split(jax.nn.sigmoid(gate_logits), N_BRANCHES, axis=-1)
    m = g_att * (att @ w_att) + g_sgu * (sgu @ w_sgu) + g_ret * (ret @ w_ret)
    return m @ w_o


def swiglu(h, w1, w3, w2):
    return (jax.nn.silu(h @ w1) * (h @ w3)) @ w2


def moe_swiglu(h, router_w, w1, w3, w2):
    logits = (h @ router_w).astype(jnp.float32)
    top_v, top_i = lax.top_k(logits, TOP_K)
    top_w = jax.nn.softmax(top_v, axis=-1)
    gates = jnp.sum(jax.nn.one_hot(top_i, N_EXPERTS, dtype=jnp.float32) * top_w[..., None], axis=-2).astype(h.dtype)
    y = jnp.zeros_like(h)
    for e in range(N_EXPERTS):
        y = y + gates[..., e:e + 1] * swiglu(h, w1[e], w3[e], w2[e])
    return y


def channel_mixer(h, i, ffn_w1, ffn_w3, ffn_w2, router_w, moe_w1, moe_w3, moe_w2):
    j = i // 2
    if i % 2 == 0:
        return swiglu(h, ffn_w1[j], ffn_w3[j], ffn_w2[j])
    return moe_swiglu(h, router_w[j], moe_w1[j], moe_w3[j], moe_w2[j])


def setup_inputs(seed: int = 0) -> dict:
    key = jax.random.key(seed)
    ks = iter(jax.random.split(key, 32))
    nrm = lambda shape, s: jax.random.normal(next(ks), shape, jnp.float32) * s
    base_logit = jnp.log(2.0 ** (5.0 + jnp.arange(RET_HEADS, dtype=jnp.float32)) - 1.0)
    d = D_MODEL
    return {
        "x": nrm((BATCH, SEQ, d), 1.0),
        "c": nrm((BATCH, d), 1.0),
        "ctx": nrm((BATCH, CTX_LEN, d), 1.0),
        "c_ctx": nrm((d,), 1.0),
        "ada_w": nrm((DEPTH, d, 6 * d), 0.5 * d ** -0.5),
        "ada_b": nrm((DEPTH, 6 * d), 0.02),
        "w_in": nrm((DEPTH, d, IN_COLS), d ** -0.5),
        "q_norm_g": 1.0 + nrm((DEPTH, HEAD_DIM), 0.02),
        "k_norm_g": 1.0 + nrm((DEPTH, HEAD_DIM), 0.02),
        "sgu_ln_g": 1.0 + nrm((DEPTH, MIX_W), 0.02),
        "sgu_ln_b": nrm((DEPTH, MIX_W), 0.02),
        "sgu_w": nrm((DEPTH, SGU_GROUPS, SGU_CHUNK, SGU_CHUNK), SGU_CHUNK ** -0.5),
        "sgu_b": 1.0 + nrm((DEPTH, SGU_GROUPS, SGU_CHUNK), 0.02),
        "ret_decay_fwd": base_logit + nrm((DEPTH, RET_HEADS), 0.1),
        "ret_decay_bwd": base_logit + nrm((DEPTH, RET_HEADS), 0.1),
        "w_br_att": nrm((DEPTH, MIX_W, d), MIX_W ** -0.5),
        "w_br_sgu": nrm((DEPTH, MIX_W, d), MIX_W ** -0.5),
        "w_br_ret": nrm((DEPTH, MIX_W, d), MIX_W ** -0.5),
        "w_out": nrm((DEPTH, d, d), BETA * d ** -0.5),
        "ln1_g": 1.0 + nrm((DEPTH, d), 0.02),
        "ln1_b": nrm((DEPTH, d), 0.02),
        "ln2_g": 1.0 + nrm((DEPTH, d), 0.02),
        "ln2_b": nrm((DEPTH, d), 0.02),
        "ffn_w1": nrm((N_DENSE, d, D_FF), d ** -0.5),
        "ffn_w3": nrm((N_DENSE, d, D_FF), d ** -0.5),
        "ffn_w2": nrm((N_DENSE, D_FF, d), BETA * D_FF ** -0.5),
        "router_w": nrm((N_MOE, d, N_EXPERTS), d ** -0.5),
        "moe_w1": nrm((N_MOE, N_EXPERTS, d, D_FF_EXPERT), d ** -0.5),
        "moe_w3": nrm((N_MOE, N_EXPERTS, d, D_FF_EXPERT), d ** -0.5),
        "moe_w2": nrm((N_MOE, N_EXPERTS, D_FF_EXPERT, d), BETA * D_FF_EXPERT ** -0.5),
    }


def reference(x, c, ctx, c_ctx, ada_w, ada_b, w_in, q_norm_g, k_norm_g, sgu_ln_g, sgu_ln_b, sgu_w, sgu_b,
              ret_decay_fwd, ret_decay_bwd, w_br_att, w_br_sgu, w_br_ret, w_out, ln1_g, ln1_b, ln2_g, ln2_b,
              ffn_w1, ffn_w3, ffn_w2, router_w, moe_w1, moe_w3, moe_w2):
    b = x.shape[0]
    ang_row, ang_col = axial_rope_angles(x.shape[1])
    zero_state = jnp.zeros((b, RET_HEADS, HEAD_DIM, HEAD_DIM), jnp.float32)
    for i in range(DEPTH):
        last = i == DEPTH - 1
        sh1_l, sc1_l, g1_l, sh2_l, sc2_l, g2_l = adaln_modulation(c, ada_w[i], ada_b[i])
        sh1_c, sc1_c, g1_c, sh2_c, sc2_c, g2_c = adaln_modulation(c_ctx, ada_w[i], ada_b[i])

        h_l = modulate(x, sh1_l, sc1_l)
        h_c = modulate(ctx, sh1_c, sc1_c)
        qa_l, ka_l, va_l, u_l, vs_l, qr_l, kr_l, vr_l, gr_l, bg_l = split_projection(h_l, w_in[i])
        qa_c, ka_c, va_c, u_c, vs_c, qr_c, kr_c, vr_c, gr_c, bg_c = split_projection(h_c, w_in[i])

        ka_c = rms_norm(to_heads(ka_c), k_norm_g[i])
        va_c = to_heads(va_c)
        qa_l = axial_rope(rms_norm(to_heads(qa_l), q_norm_g[i]), ang_row, ang_col)
        ka_l = axial_rope(rms_norm(to_heads(ka_l), k_norm_g[i]), ang_row, ang_col)
        k_all = jnp.concatenate([ka_c, ka_l], axis=1)
        v_all = jnp.concatenate([va_c, to_heads(va_l)], axis=1)
        att_l = blocked_gqa(qa_l, k_all, v_all)

        sgu_l = spatial_gating(u_l, vs_l, sgu_ln_g[i], sgu_ln_b[i], sgu_w[i], sgu_b[i])

        lg_f = jax.nn.log_sigmoid(ret_decay_fwd[i].astype(jnp.float32))
        lg_b = jax.nn.log_sigmoid(ret_decay_bwd[i].astype(jnp.float32))
        qr_c, kr_c, vr_c = retention_heads(qr_c, kr_c, vr_c)
        if last:
            s_f = retention_state(kr_c, vr_c, lg_f)
            s_b = retention_state(jnp.flip(kr_c, 1), jnp.flip(vr_c, 1), lg_b)
        else:
            ret_c, s_f, s_b = bidirectional_retention(qr_c, kr_c, vr_c, lg_f, lg_b, zero_state, zero_state)
        qr_l, kr_l, vr_l = retention_heads(qr_l, kr_l, vr_l)
        ret_l, _, _ = bidirectional_retention(qr_l, kr_l, vr_l, lg_f, lg_b, s_f, s_b)
        ret_l = retention_output(ret_l, gr_l)

        mix_l = merge_branches(att_l, sgu_l, ret_l, bg_l, w_br_att[i], w_br_sgu[i], w_br_ret[i], w_out[i])
        x_new = layer_norm(ALPHA * x + g1_l * mix_l, ln1_g[i], ln1_b[i])
        f_l = channel_mixer(modulate(x_new, sh2_l, sc2_l), i, ffn_w1, ffn_w3, ffn_w2, router_w, moe_w1, moe_w3, moe_w2)
        x_new = layer_norm(ALPHA * x_new + g2_l * f_l, ln2_g[i], ln2_b[i])

        if not last:
            qa_c = rms_norm(to_heads(qa_c), q_norm_g[i])
            att_c = blocked_gqa(qa_c, ka_c, va_c)
            sgu_c = spatial_gating(u_c, vs_c, sgu_ln_g[i], sgu_ln_b[i], sgu_w[i], sgu_b[i])
            ret_c = retention_output(ret_c, gr_c)
            mix_c = merge_branches(att_c, sgu_c, ret_c, bg_c, w_br_att[i], w_br_sgu[i], w_br_ret[i], w_out[i])
            ctx = layer_norm(ALPHA * ctx + g1_c * mix_c, ln1_g[i], ln1_b[i])
            f_c = channel_mixer(modulate(ctx, sh2_c, sc2_c), i, ffn_w1, ffn_w3, ffn_w2, router_w, moe_w1, moe_w3, moe_w2)
            ctx = layer_norm(ALPHA * ctx + g2_c * f_c, ln2_g[i], ln2_b[i])
        x = x_new
    return x
```

```python
import functools
import math

import jax
import jax.numpy as jnp
from jax import lax
from jax.experimental import pallas as pl
from jax.experimental.pallas import tpu as pltpu

F32 = jnp.float32
BF16 = jnp.bfloat16
EPS = 1e-6
GRID_W = 64
ROPE_THETA = 10000.0
GQA_GROUP = 4
TOP_K = 2
N_MODS = 6
MOD_ROWS = 8
V7X_VMEM_BYTES = 64 * 1024 * 1024
VMEM_LIMIT = V7X_VMEM_BYTES - 8 * 1024 * 1024
LOG2E = 1.4426950408889634


def _cparams(*sem):
    return pltpu.CompilerParams(dimension_semantics=sem, vmem_limit_bytes=VMEM_LIMIT)


def _resident(shape, index_map):
    return pl.BlockSpec(shape, index_map, pipeline_mode=pl.Buffered(1))


def _largest_divisor(n, candidates):
    for c in candidates:
        if n % c == 0:
            return c
    raise ValueError(f"no tile in {candidates} divides {n}")


def _ln(x):
    mu = jnp.mean(x, axis=-1, keepdims=True)
    xc = x - mu
    var = jnp.mean(xc * xc, axis=-1, keepdims=True)
    return xc * lax.rsqrt(var + EPS)


def _silu(x):
    return x * jax.nn.sigmoid(x)


def _gelu_tanh(x):
    return 0.5 * x * (1.0 + jnp.tanh(math.sqrt(2.0 / math.pi) * (x + 0.044715 * (x * x * x))))


def _ada_kernel(c_ref, w_ref, b_ref, o_ref):
    s = _silu(c_ref[...]).astype(BF16)
    o_ref[...] = jnp.dot(s, w_ref[...].astype(BF16), preferred_element_type=F32) + b_ref[...]


def _ada_mods(cond, ada_w, ada_b):
    depth, d, nd = ada_w.shape
    tn = _largest_divisor(nd, (1024, 512, 256, 128))
    out = pl.pallas_call(
        _ada_kernel,
        grid=(depth, nd // tn),
        in_specs=[
            pl.BlockSpec((MOD_ROWS, d), lambda l, j: (0, 0)),
            pl.BlockSpec((None, d, tn), lambda l, j: (l, 0, j)),
            pl.BlockSpec((None, 1, tn), lambda l, j: (l, 0, j)),
        ],
        out_specs=pl.BlockSpec((None, MOD_ROWS, tn), lambda l, j: (l, 0, j)),
        out_shape=jax.ShapeDtypeStruct((depth, MOD_ROWS, nd), F32),
        compiler_params=_cparams("parallel", "parallel"),
        name="ada_mods",
    )(cond, ada_w, ada_b.reshape(depth, 1, nd))
    return out.reshape(depth * MOD_ROWS * N_MODS, 1, d)


class _Dims:
    def __init__(self, x, ctx, ada_w, q_norm_g, sgu_w, ret_decay_fwd, ffn_w1, moe_w1):
        self.b, self.l, self.d = x.shape
        self.ctx = ctx.shape[1]
        self.depth = ada_w.shape[0]
        self.hd = q_norm_g.shape[1]
        self.mw = self.d // 2
        self.nq = self.mw // self.hd
        self.nkv = self.nq // GQA_GROUP
        self.kvw = self.nkv * self.hd
        self.sgu_groups, self.sgu_chunk = sgu_w.shape[1], sgu_w.shape[2]
        self.ret_heads = ret_decay_fwd.shape[1]
        self.ret_chunk = 128
        self.n_lat = self.b * self.l
        self.n_ctx = self.b * self.ctx
        self.n_tok = self.n_lat + self.n_ctx
        self.lk = self.ctx + self.l
        self.n_exp = moe_w1.shape[1]
        self.alpha = (2 * self.depth) ** 0.25
        mw = self.mw
        self.c_u, self.c_v = 0, mw
        self.c_rq, self.c_rk, self.c_rv, self.c_rg = 2 * mw, 3 * mw, 4 * mw, 5 * mw
        self.c_bg = 6 * mw
        self.c_aq = 6 * mw + 3 * self.d
        self.c_ak = self.c_aq + mw
        self.c_av = self.c_ak + self.kvw
        self.n_cols = self.c_av + self.kvw

    def mod_row(self, layer, j):
        def f(i, tm):
            bidx = jnp.minimum((i * tm) // self.l, self.b)
            return (layer * MOD_ROWS + bidx) * N_MODS + j
        return f


def _reorder_w_in(w, dm):
    mw, kvw = dm.mw, dm.kvw
    aq = w[:, :mw]
    ak = w[:, mw:mw + kvw]
    av = w[:, mw + kvw:mw + 2 * kvw]
    rest = w[:, mw + 2 * kvw:]
    return jnp.concatenate([rest, aq, ak, av], axis=1).astype(BF16)


def _proj_kernel(x_ref, sh_ref, sc_ref, w_ref, o_ref, h_ref):
    @pl.when(pl.program_id(1) == 0)
    def _():
        h = _ln(x_ref[...]) * (1.0 + sc_ref[...]) + sh_ref[...]
        h_ref[...] = h.astype(BF16)

    o_ref[...] = jnp.dot(h_ref[...], w_ref[...], preferred_element_type=F32).astype(o_ref.dtype)


def _in_projection(xs, mods, w, dm, layer):
    t, d = xs.shape
    nc = w.shape[1]
    tm = _largest_divisor(math.gcd(dm.l, dm.n_ctx), (1024, 512, 256, 128))
    tn = _largest_divisor(nc, (dm.mw // 2,))
    row = dm.mod_row
    return pl.pallas_call(
        _proj_kernel,
        grid=(t // tm, nc // tn),
        in_specs=[
            pl.BlockSpec((tm, d), lambda i, j: (i, 0)),
            pl.BlockSpec((None, 1, d), lambda i, j: (row(layer, 0)(i, tm), 0, 0)),
            pl.BlockSpec((None, 1, d), lambda i, j: (row(layer, 1)(i, tm), 0, 0)),
            pl.BlockSpec((d, tn), lambda i, j: (0, j)),
        ],
        out_specs=pl.BlockSpec((tm, tn), lambda i, j: (i, j)),
        out_shape=jax.ShapeDtypeStruct((t, nc), BF16),
        scratch_shapes=[pltpu.VMEM((tm, d), BF16)],
        compiler_params=_cparams("parallel", "arbitrary"),
        name="in_projection",
    )(xs, mods, mods, w)


def _rope_tables(dm):
    n_freq = dm.hd // 4
    t = jnp.arange(dm.l, dtype=F32)
    row = jnp.floor(t / GRID_W)
    col = t - row * GRID_W
    inv_freq = ROPE_THETA ** (-jnp.arange(n_freq, dtype=F32) / n_freq)
    ar, ac = row[:, None] * inv_freq, col[:, None] * inv_freq
    z = jnp.zeros_like(ar)
    cos = jnp.concatenate([jnp.cos(ar), jnp.cos(ar), jnp.cos(ac), jnp.cos(ac)], axis=1)
    sin_lo = jnp.concatenate([z, jnp.sin(ar), z, jnp.sin(ac)], axis=1)
    sin_hi = jnp.concatenate([-jnp.sin(ar), z, -jnp.sin(ac), z], axis=1)
    pad = lambda a, v: jnp.concatenate([a, jnp.full((dm.ctx, dm.hd), v, F32)], axis=0)
    return pad(cos, 1.0), pad(sin_lo, 0.0), pad(sin_hi, 0.0)


def _prep_kernel(q_ref, k_ref, v_ref, qg_ref, kg_ref, cos_ref, slo_ref, shi_ref, qo_ref, kt_ref, vo_ref,
                 *, nq, nkv, hd, qscale):
    cos, slo, shi = cos_ref[...], slo_ref[...], shi_ref[...]
    quarter = hd // 4

    def norm_rope(xh, g):
        y = xh * lax.rsqrt(jnp.mean(xh * xh, axis=-1, keepdims=True) + EPS) * g
        return y * cos + pltpu.roll(y, quarter, 1) * slo + pltpu.roll(y, hd - quarter, 1) * shi

    qg = qg_ref[...] * qscale
    kg = kg_ref[...]
    for h in range(nq):
        sl = slice(h * hd, (h + 1) * hd)
        qo_ref[:, sl] = norm_rope(q_ref[:, sl].astype(F32), qg).astype(qo_ref.dtype)
    for g in range(nkv):
        sl = slice(g * hd, (g + 1) * hd)
        kt_ref[g] = norm_rope(k_ref[:, sl].astype(F32), kg).T.astype(kt_ref.dtype)
        vo_ref[g] = v_ref[:, sl]


def _attention_prep(proj, q_norm_g, k_norm_g, tables, dm):
    t = proj.shape[0]
    tp = _largest_divisor(math.gcd(dm.l, dm.ctx), (256, 128))
    mw, kvw, hd = dm.mw, dm.kvw, dm.hd
    n_lat_tiles = dm.n_lat // tp

    def batch_of(i):
        return jnp.where(i < n_lat_tiles, (i * tp) // dm.l, (i * tp - dm.n_lat) // dm.ctx)

    def key_block(i):
        lat = (dm.ctx + (i * tp) % dm.l) // tp
        ctx = ((i * tp - dm.n_lat) % dm.ctx) // tp
        return jnp.where(i < n_lat_tiles, lat, ctx)

    def table_block(i):
        lat = ((i * tp) % dm.l) // tp
        ctx = dm.l // tp + ((i * tp - dm.n_lat) % dm.ctx) // tp
        return jnp.where(i < n_lat_tiles, lat, ctx)

    tab_spec = pl.BlockSpec((tp, hd), lambda i: (table_block(i), 0))
    qscale = hd ** -0.5 * LOG2E
    return pl.pallas_call(
        functools.partial(_prep_kernel, nq=dm.nq, nkv=dm.nkv, hd=hd, qscale=qscale),
        grid=(t // tp,),
        in_specs=[
            pl.BlockSpec((tp, mw), lambda i: (i, dm.c_aq // mw)),
            pl.BlockSpec((tp, kvw), lambda i: (i, dm.c_ak // kvw)),
            pl.BlockSpec((tp, kvw), lambda i: (i, dm.c_av // kvw)),
            pl.BlockSpec((1, hd), lambda i: (0, 0)),
            pl.BlockSpec((1, hd), lambda i: (0, 0)),
            tab_spec, tab_spec, tab_spec,
        ],
        out_specs=[
            pl.BlockSpec((tp, mw), lambda i: (i, 0)),
            pl.BlockSpec((None, dm.nkv, hd, tp), lambda i: (batch_of(i), 0, 0, key_block(i))),
            pl.BlockSpec((None, dm.nkv, tp, hd), lambda i: (batch_of(i), 0, key_block(i), 0)),
        ],
        out_shape=[
            jax.ShapeDtypeStruct((t, mw), BF16),
            jax.ShapeDtypeStruct((dm.b, dm.nkv, hd, dm.lk), BF16),
            jax.ShapeDtypeStruct((dm.b, dm.nkv, dm.lk, hd), BF16),
        ],
        compiler_params=_cparams("parallel"),
        name="attention_prep",
    )(proj, proj, proj, q_norm_g.reshape(1, hd), k_norm_g.reshape(1, hd), *tables)


def _attn_kernel(q_ref, kt_ref, v_ref, o_ref, m_ref, l_ref, acc_ref, *, hd, n_lat_tiles, lat_keys, ctx_keys):
    tq = q_ref.shape[0]
    q = q_ref[...]
    qs = jnp.concatenate([q[:, h * hd:(h + 1) * hd] for h in range(GQA_GROUP)], axis=0)

    def attend(tk, nk):
        m_ref[...] = jnp.full(m_ref.shape, -jnp.inf, F32)
        l_ref[...] = jnp.zeros(l_ref.shape, F32)
        acc_ref[...] = jnp.zeros(acc_ref.shape, F32)

        def body(kt, carry):
            off = pl.multiple_of(kt * tk, tk)
            s = jnp.dot(qs, kt_ref[:, pl.ds(off, tk)], preferred_element_type=F32)
            m_prev = m_ref[...]
            m_new = jnp.maximum(m_prev, jnp.max(s, axis=-1, keepdims=True))
            p = jnp.exp2(s - m_new)
            a = jnp.exp2(m_prev - m_new)
            l_ref[...] = a * l_ref[...] + jnp.sum(p, axis=-1, keepdims=True)
            acc_ref[...] = a * acc_ref[...] + jnp.dot(p.astype(BF16), v_ref[pl.ds(off, tk), :],
                                                      preferred_element_type=F32)
            m_ref[...] = m_new
            return carry

        lax.fori_loop(0, nk, body, 0)
        o = acc_ref[...] * (1.0 / l_ref[...])
        o_ref[...] = jnp.concatenate([o[h * tq:(h + 1) * tq] for h in range(GQA_GROUP)],
                                     axis=1).astype(o_ref.dtype)

    is_latent = pl.program_id(2) < n_lat_tiles

    @pl.when(is_latent)
    def _():
        attend(*lat_keys)

    if ctx_keys is not None:
        @pl.when(jnp.logical_not(is_latent))
        def _():
            attend(*ctx_keys)


def _attention(qp, kt, v, dm, *, with_ctx):
    hd, mw = dm.hd, dm.mw
    tq = _largest_divisor(math.gcd(dm.l, dm.ctx), (256, 128))
    key_tiling = lambda n: (lambda tk: (tk, n // tk))(_largest_divisor(n, (768, 512, 256, 128)))
    gw = GQA_GROUP * hd
    nl, nc = dm.l // tq, dm.ctx // tq
    rows = dm.n_tok if with_ctx else dm.n_lat

    def q_map(b, g, i):
        return jnp.where(i < nl, b * nl + i, dm.n_lat // tq + b * nc + (i - nl)), g

    return pl.pallas_call(
        functools.partial(_attn_kernel, hd=hd, n_lat_tiles=nl, lat_keys=key_tiling(dm.lk),
                          ctx_keys=key_tiling(dm.ctx) if with_ctx else None),
        grid=(dm.b, dm.nkv, nl + (nc if with_ctx else 0)),
        in_specs=[
            pl.BlockSpec((tq, gw), q_map),
            pl.BlockSpec((None, None, hd, dm.lk), lambda b, g, i: (b, g, 0, 0)),
            pl.BlockSpec((None, None, dm.lk, hd), lambda b, g, i: (b, g, 0, 0)),
        ],
        out_specs=pl.BlockSpec((tq, gw), q_map),
        out_shape=jax.ShapeDtypeStruct((rows, mw), BF16),
        scratch_shapes=[
            pltpu.VMEM((GQA_GROUP * tq, 1), F32),
            pltpu.VMEM((GQA_GROUP * tq, 1), F32),
            pltpu.VMEM((GQA_GROUP * tq, hd), F32),
        ],
        compiler_params=_cparams("parallel", "parallel", "arbitrary"),
        name="attention",
    )(qp, kt, v)


def _sgu_kernel(u_ref, v_ref, g_ref, b_ref, w_ref, bs_ref, o_ref, *, ng, gw, chunk):
    ts = u_ref.shape[0]
    u = _gelu_tanh(u_ref[...].astype(F32))
    v = _ln(_gelu_tanh(v_ref[...].astype(F32))) * g_ref[...] + b_ref[...]
    vb = v.astype(BF16)
    for c in range(ts // chunk):
        rows = slice(c * chunk, (c + 1) * chunk)
        for g in range(ng):
            cols = slice(g * gw, (g + 1) * gw)
            z = jnp.dot(w_ref[g], vb[rows, cols], preferred_element_type=F32) + bs_ref[g]
            o_ref[rows, cols] = (u[rows, cols] * z).astype(o_ref.dtype)


def _sgu(proj, ln_g, ln_b, w_s, b_s, dm, rows):
    mw, chunk, ng = dm.mw, dm.sgu_chunk, dm.sgu_groups
    ts = _largest_divisor(math.gcd(dm.l, dm.n_ctx), (256, 128))
    return pl.pallas_call(
        functools.partial(_sgu_kernel, ng=ng, gw=mw // ng, chunk=chunk),
        grid=(rows // ts,),
        in_specs=[
            pl.BlockSpec((ts, mw), lambda i: (i, dm.c_u // mw)),
            pl.BlockSpec((ts, mw), lambda i: (i, dm.c_v // mw)),
            pl.BlockSpec((1, mw), lambda i: (0, 0)),
            pl.BlockSpec((1, mw), lambda i: (0, 0)),
            pl.BlockSpec((ng, chunk, chunk), lambda i: (0, 0, 0)),
            pl.BlockSpec((ng, chunk, 1), lambda i: (0, 0, 0)),
        ],
        out_specs=pl.BlockSpec((ts, mw), lambda i: (i, 0)),
        out_shape=jax.ShapeDtypeStruct((rows, mw), BF16),
        compiler_params=_cparams("parallel"),
        name="sgu",
    )(proj, proj, ln_g.reshape(1, mw), ln_b.reshape(1, mw), w_s.astype(BF16), b_s.reshape(ng, chunk, 1))


def _ret_tables(decay_logit, dm, backward):
    c, hd, nh = dm.ret_chunk, dm.hd, dm.ret_heads
    lg = jax.nn.log_sigmoid(decay_logit.astype(F32))
    pos = jnp.arange(c, dtype=F32)
    diff = pos[:, None] - pos[None, :]
    if backward:
        diff = -diff
        xi_e = c - pos
        zeta_e = pos
    else:
        xi_e = pos + 1.0
        zeta_e = c - 1.0 - pos
    dmat = jnp.where(diff >= 0, jnp.exp(jnp.maximum(diff, 0.0)[None] * lg[:, None, None]), 0.0)
    dmat = dmat * hd ** -0.5
    xi = jnp.exp(xi_e[None, :] * lg[:, None])[..., None]
    zeta = (jnp.exp(zeta_e[None, :] * lg[:, None]) * hd ** -0.5)[..., None]
    dec = jnp.broadcast_to(jnp.exp(c * lg)[:, None, None], (nh, 1, hd))
    return dmat.astype(F32), xi.astype(F32), zeta.astype(F32), dec.astype(F32)


def _ret_kernel(*refs, nh, hd, final):
    if final:
        q_ref, k_ref, v_ref, dm_ref, xi_ref, zeta_ref, dec_ref, of_ref, g_ref, o_ref, st_ref = refs
    else:
        q_ref, k_ref, v_ref, dm_ref, xi_ref, zeta_ref, dec_ref, o_ref, st_ref = refs

    @pl.when(pl.program_id(1) == 0)
    def _():
        st_ref[...] = jnp.zeros(st_ref.shape, F32)

    for h in range(nh):
        sl = slice(h * hd, (h + 1) * hd)
        qh, kh, vh = q_ref[:, sl], k_ref[:, sl], v_ref[:, sl]
        s = lax.dot_general(qh, kh, (((1,), (1,)), ((), ())), preferred_element_type=F32)
        inner = (s * dm_ref[h]).astype(BF16)
        st = st_ref[h]
        o = jnp.dot(inner, vh, preferred_element_type=F32)
        o = o + jnp.dot(qh, st.astype(BF16), preferred_element_type=F32) * xi_ref[h]
        kz = (kh.astype(F32) * zeta_ref[h]).astype(BF16)
        st_ref[h] = st * dec_ref[h] + lax.dot_general(kz, vh, (((0,), (0,)), ((), ())),
                                                      preferred_element_type=F32)
        if final:
            gn = _ln(of_ref[:, sl] + o)
            o_ref[:, sl] = (_silu(g_ref[:, sl].astype(F32)) * gn).astype(o_ref.dtype)
        else:
            o_ref[:, sl] = o


def _retention_pass(proj, tables, dm, *, backward, o_fwd=None):
    c, mw, nh, hd = dm.ret_chunk, dm.mw, dm.ret_heads, dm.hd
    n_cc, n_lc = dm.ctx // c, dm.l // c
    steps = n_cc + n_lc

    def row_block(b, t):
        if backward:
            ctx = dm.n_lat // c + b * n_cc + (n_cc - 1 - t)
            lat = b * n_lc + (n_lc - 1 - (t - n_cc))
        else:
            ctx = dm.n_lat // c + b * n_cc + t
            lat = b * n_lc + (t - n_cc)
        return jnp.where(t < n_cc, ctx, lat)

    col = lambda off: (lambda b, t: (row_block(b, t), off // mw))
    const3 = lambda b, t: (0, 0, 0)
    in_specs = [
        pl.BlockSpec((c, mw), col(dm.c_rq)),
        pl.BlockSpec((c, mw), col(dm.c_rk)),
        pl.BlockSpec((c, mw), col(dm.c_rv)),
        pl.BlockSpec((nh, c, c), const3),
        pl.BlockSpec((nh, c, 1), const3),
        pl.BlockSpec((nh, c, 1), const3),
        pl.BlockSpec((nh, 1, hd), const3),
    ]
    args = [proj, proj, proj, *tables]
    final = o_fwd is not None
    if final:
        in_specs += [pl.BlockSpec((c, mw), col(0)), pl.BlockSpec((c, mw), col(dm.c_rg))]
        args += [o_fwd, proj]
    return pl.pallas_call(
        functools.partial(_ret_kernel, nh=nh, hd=hd, final=final),
        grid=(dm.b, steps),
        in_specs=in_specs,
        out_specs=pl.BlockSpec((c, mw), col(0)),
        out_shape=jax.ShapeDtypeStruct((dm.n_tok, mw), BF16 if final else F32),
        scratch_shapes=[pltpu.VMEM((nh, hd, hd), F32)],
        compiler_params=_cparams("parallel", "arbitrary"),
        name="retention_bwd" if final else "retention_fwd",
    )(*args)


def _merge_kernel(att_ref, sgu_ref, ret_ref, ga_ref, gs_ref, gr_ref, wa_ref, ws_ref, wr_ref, wo_ref,
                  x_ref, g1_ref, lng_ref, lnb_ref, o_ref, *, alpha):
    def branch(a_ref, w_ref, gate_ref):
        return jax.nn.sigmoid(gate_ref[...].astype(F32)) * jnp.dot(a_ref[...], w_ref[...],
                                                                    preferred_element_type=F32)

    m = branch(att_ref, wa_ref, ga_ref) + branch(sgu_ref, ws_ref, gs_ref) + branch(ret_ref, wr_ref, gr_ref)
    mix = jnp.dot(m.astype(BF16), wo_ref[...], preferred_element_type=F32)
    y = alpha * x_ref[...] + g1_ref[...] * mix
    o_ref[...] = _ln(y) * lng_ref[...] + lnb_ref[...]


def _merge(att, sgu, ret, proj, w_att, w_sgu, w_ret, w_o, xs, mods, ln_g, ln_b, dm, layer, rows):
    d, mw = dm.d, dm.mw
    tm = _largest_divisor(math.gcd(dm.l, dm.n_ctx), (256, 128))
    row = dm.mod_row
    tok = lambda w: pl.BlockSpec((tm, w), lambda i: (i, 0))
    gate = lambda k: pl.BlockSpec((tm, d), lambda i: (i, dm.c_bg // d + k))
    vec = pl.BlockSpec((1, d), lambda i: (0, 0))
    return pl.pallas_call(
        functools.partial(_merge_kernel, alpha=dm.alpha),
        grid=(rows // tm,),
        in_specs=[
            tok(mw), tok(mw), tok(mw), gate(0), gate(1), gate(2),
            _resident((mw, d), lambda i: (0, 0)), _resident((mw, d), lambda i: (0, 0)),
            _resident((mw, d), lambda i: (0, 0)), _resident((d, d), lambda i: (0, 0)),
            tok(d),
            pl.BlockSpec((None, 1, d), lambda i: (row(layer, 2)(i, tm), 0, 0)),
            vec, vec,
        ],
        out_specs=tok(d),
        out_shape=jax.ShapeDtypeStruct((rows, d), F32),
        compiler_params=_cparams("parallel"),
        name="merge",
    )(att, sgu, ret, proj, proj, proj, w_att, w_sgu, w_ret, w_o, xs, mods,
      ln_g.reshape(1, d), ln_b.reshape(1, d))


def _ffn_kernel(x_ref, sh_ref, sc_ref, w1_ref, w3_ref, w2_ref, g2_ref, lng_ref, lnb_ref, o_ref,
                h_ref, acc_ref, *, alpha):
    j = pl.program_id(1)

    @pl.when(j == 0)
    def _():
        h = _ln(x_ref[...]) * (1.0 + sc_ref[...]) + sh_ref[...]
        h_ref[...] = h.astype(BF16)
        acc_ref[...] = jnp.zeros(acc_ref.shape, F32)

    h = h_ref[...]
    a = jnp.dot(h, w1_ref[...], preferred_element_type=F32)
    b = jnp.dot(h, w3_ref[...], preferred_element_type=F32)
    acc_ref[...] += jnp.dot((_silu(a) * b).astype(BF16), w2_ref[...], preferred_element_type=F32)

    @pl.when(j == pl.num_programs(1) - 1)
    def _():
        y = alpha * x_ref[...] + g2_ref[...] * acc_ref[...]
        o_ref[...] = _ln(y) * lng_ref[...] + lnb_ref[...]


def _ffn(xs, mods, w1, w3, w2, ln_g, ln_b, dm, layer):
    t, d = xs.shape
    f = w1.shape[1]
    tm = _largest_divisor(math.gcd(dm.l, dm.n_ctx), (512, 256, 128))
    tf = _largest_divisor(f, (512, 256, 128))
    row = dm.mod_row
    mod = lambda k: pl.BlockSpec((None, 1, d), lambda i, j: (row(layer, k)(i, tm), 0, 0))
    vec = pl.BlockSpec((1, d), lambda i, j: (0, 0))
    return pl.pallas_call(
        functools.partial(_ffn_kernel, alpha=dm.alpha),
        grid=(t // tm, f // tf),
        in_specs=[
            pl.BlockSpec((tm, d), lambda i, j: (i, 0)),
            mod(3), mod(4),
            pl.BlockSpec((d, tf), lambda i, j: (0, j)),
            pl.BlockSpec((d, tf), lambda i, j: (0, j)),
            pl.BlockSpec((tf, d), lambda i, j: (j, 0)),
            mod(5), vec, vec,
        ],
        out_specs=pl.BlockSpec((tm, d), lambda i, j: (i, 0)),
        out_shape=jax.ShapeDtypeStruct((t, d), F32),
        scratch_shapes=[pltpu.VMEM((tm, d), BF16), pltpu.VMEM((tm, d), F32)],
        compiler_params=_cparams("parallel", "arbitrary"),
        name="ffn_dense",
    )(xs, mods, mods, w1, w3, w2, mods, ln_g.reshape(1, d), ln_b.reshape(1, d))


def _router_kernel(x_ref, sh_ref, sc_ref, whi_ref, wlo_ref, h_ref, gates_ref, code_ref):
    h = _ln(x_ref[...]) * (1.0 + sc_ref[...]) + sh_ref[...]
    h_ref[...] = h
    h_hi = h.astype(BF16)
    h_lo = (h - h_hi.astype(F32)).astype(BF16)
    w_hi = whi_ref[...]
    logits = (jnp.dot(h_hi, w_hi, preferred_element_type=F32)
              + jnp.dot(h_lo, w_hi, preferred_element_type=F32)
              + jnp.dot(h_hi, wlo_ref[...], preferred_element_type=F32))
    n_exp = float(logits.shape[-1])
    ids = lax.broadcasted_iota(jnp.int32, logits.shape, 1).astype(F32)
    m1 = jnp.max(logits, axis=-1, keepdims=True)
    i1 = jnp.min(jnp.where(logits == m1, ids, n_exp), axis=-1, keepdims=True)
    rest = jnp.where(ids == i1, -jnp.inf, logits)
    m2 = jnp.max(rest, axis=-1, keepdims=True)
    i2 = jnp.min(jnp.where(rest == m2, ids, n_exp), axis=-1, keepdims=True)
    e = jnp.exp(m2 - m1)
    w1 = 1.0 / (1.0 + e)
    w2 = e * w1
    gates_ref[...] = jnp.where(ids == i1, w1, jnp.where(ids == i2, w2, 0.0))
    code_ref[...] = jnp.where(ids == i1, 1, jnp.where(ids == i2, 2, 0)).astype(jnp.int32)


def _router(xs, mods, router_w, dm, layer, rows):
    d, n_exp = router_w.shape
    tm = _largest_divisor(dm.l, (512, 256, 128))
    row = dm.mod_row
    mod = lambda k: pl.BlockSpec((None, 1, d), lambda i: (row(layer, k)(i, tm), 0, 0))
    w_hi = router_w.astype(BF16)
    w_lo = (router_w - w_hi.astype(F32)).astype(BF16)
    return pl.pallas_call(
        _router_kernel,
        grid=(rows // tm,),
        in_specs=[
            pl.BlockSpec((tm, d), lambda i: (i, 0)), mod(3), mod(4),
            pl.BlockSpec((d, n_exp), lambda i: (0, 0)),
            pl.BlockSpec((d, n_exp), lambda i: (0, 0)),
        ],
        out_specs=[
            pl.BlockSpec((tm, d), lambda i: (i, 0)),
            pl.BlockSpec((tm, n_exp), lambda i: (i, 0)),
            pl.BlockSpec((tm, n_exp), lambda i: (i, 0)),
        ],
        out_shape=[
            jax.ShapeDtypeStruct((rows, d), F32),
            jax.ShapeDtypeStruct((rows, n_exp), F32),
            jax.ShapeDtypeStruct((rows, n_exp), jnp.int32),
        ],
        compiler_params=_cparams("parallel"),
        name="router",
    )(xs, mods, mods, w_hi, w_lo)


def _gather_kernel(idx_ref, src_ref, o_ref, sem):
    rt = o_ref.shape[0]

    def row_copy(r):
        return pltpu.make_async_copy(src_ref.at[pl.ds(idx_ref[0, 0, r], 1), :], o_ref.at[pl.ds(r, 1), :], sem)

    def start(r, carry):
        row_copy(r).start()
        return carry

    def wait(r, carry):
        row_copy(r).wait()
        return carry

    lax.fori_loop(0, rt, start, 0)
    lax.fori_loop(0, rt, wait, 0)


def _gather_rows(src, idx):
    m = idx.shape[0]
    d = src.shape[1]
    rt = _largest_divisor(m, (256, 128))
    return pl.pallas_call(
        _gather_kernel,
        grid=(m // rt,),
        in_specs=[
            pl.BlockSpec((1, 1, rt), lambda i: (i, 0, 0), memory_space=pltpu.SMEM),
            pl.BlockSpec(memory_space=pl.ANY),
        ],
        out_specs=pl.BlockSpec((rt, d), lambda i: (i, 0)),
        out_shape=jax.ShapeDtypeStruct((m, d), src.dtype),
        scratch_shapes=[pltpu.SemaphoreType.DMA(())],
        compiler_params=_cparams("arbitrary"),
        name="gather_rows",
    )(idx.reshape(m // rt, 1, rt), src)


def _expert_kernel(te_ref, tv_ref, x_ref, w1_ref, w3_ref, w2_ref, o_ref, h_ref, acc_ref):
    i, j = pl.program_id(0), pl.program_id(1)

    @pl.when(tv_ref[i] > 0)
    def _():
        @pl.when(j == 0)
        def _():
            h_ref[...] = x_ref[...].astype(BF16)
            acc_ref[...] = jnp.zeros(acc_ref.shape, F32)

        h = h_ref[...]
        a = jnp.dot(h, w1_ref[...], preferred_element_type=F32)
        b = jnp.dot(h, w3_ref[...], preferred_element_type=F32)
        acc_ref[...] += jnp.dot((_silu(a) * b).astype(BF16), w2_ref[...], preferred_element_type=F32)

        @pl.when(j == pl.num_programs(1) - 1)
        def _():
            o_ref[...] = acc_ref[...]

    @pl.when(jnp.logical_and(tv_ref[i] == 0, j == 0))
    def _():
        o_ref[...] = jnp.zeros(o_ref.shape, o_ref.dtype)


def _expert_mlp(xg, tile_expert, tile_valid, w1, w3, w2, tm):
    s, d = xg.shape
    f = w1.shape[2]
    tf = _largest_divisor(f, (512, 256, 128))
    nj = f // tf
    jj = lambda i, j, tv: jnp.where(tv[i] > 0, j, nj - 1)
    grid_spec = pltpu.PrefetchScalarGridSpec(
        num_scalar_prefetch=2,
        grid=(s // tm, nj),
        in_specs=[
            pl.BlockSpec((tm, d), lambda i, j, te, tv: (i, 0)),
            pl.BlockSpec((None, d, tf), lambda i, j, te, tv: (te[i], 0, jj(i, j, tv))),
            pl.BlockSpec((None, d, tf), lambda i, j, te, tv: (te[i], 0, jj(i, j, tv))),
            pl.BlockSpec((None, tf, d), lambda i, j, te, tv: (te[i], jj(i, j, tv), 0)),
        ],
        out_specs=pl.BlockSpec((tm, d), lambda i, j, te, tv: (i, 0)),
        scratch_shapes=[pltpu.VMEM((tm, d), BF16), pltpu.VMEM((tm, d), F32)],
    )
    return pl.pallas_call(
        _expert_kernel,
        grid_spec=grid_spec,
        out_shape=jax.ShapeDtypeStruct((s, d), F32),
        compiler_params=_cparams("parallel", "arbitrary"),
        name="expert_mlp",
    )(tile_expert, tile_valid, xg, w1, w3, w2)


def _moe_out_kernel(y1_ref, y2_ref, gates_ref, code_ref, x_ref, g2_ref, lng_ref, lnb_ref, o_ref, *, alpha):
    gates, code = gates_ref[...], code_ref[...]
    w1 = jnp.sum(jnp.where(code == 1, gates, 0.0), axis=-1, keepdims=True)
    w2 = jnp.sum(jnp.where(code == 2, gates, 0.0), axis=-1, keepdims=True)
    f = w1 * y1_ref[...] + w2 * y2_ref[...]
    y = alpha * x_ref[...] + g2_ref[...] * f
    o_ref[...] = _ln(y) * lng_ref[...] + lnb_ref[...]


def _moe_out(yc, gates, code, xs, mods, ln_g, ln_b, dm, layer, rows):
    d, n_exp = dm.d, gates.shape[1]
    tm = _largest_divisor(dm.l, (512, 256, 128))
    nt = rows // tm
    row = dm.mod_row
    vec = pl.BlockSpec((1, d), lambda i: (0, 0))
    return pl.pallas_call(
        functools.partial(_moe_out_kernel, alpha=dm.alpha),
        grid=(nt,),
        in_specs=[
            pl.BlockSpec((tm, d), lambda i: (i, 0)),
            pl.BlockSpec((tm, d), lambda i: (nt + i, 0)),
            pl.BlockSpec((tm, n_exp), lambda i: (i, 0)),
            pl.BlockSpec((tm, n_exp), lambda i: (i, 0)),
            pl.BlockSpec((tm, d), lambda i: (i, 0)),
            pl.BlockSpec((None, 1, d), lambda i: (row(layer, 5)(i, tm), 0, 0)),
            vec, vec,
        ],
        out_specs=pl.BlockSpec((tm, d), lambda i: (i, 0)),
        out_shape=jax.ShapeDtypeStruct((rows, d), F32),
        compiler_params=_cparams("parallel"),
        name="moe_out",
    )(yc, yc, gates, code, xs, mods, ln_g.reshape(1, d), ln_b.reshape(1, d))


def _moe(xs, mods, router_w, w1, w3, w2, ln_g, ln_b, dm, layer, rows):
    n_exp = router_w.shape[1]
    tm = 512 if rows * TOP_K >= 512 * n_exp * 4 else 128
    h, gates, code = _router(xs, mods, router_w, dm, layer, rows)

    onehot = jnp.concatenate([code == 1, code == 2], axis=0).astype(jnp.int32)
    csum = jnp.cumsum(onehot, axis=0)
    counts = csum[-1]
    rank = jnp.sum(csum * onehot, axis=1) - 1
    expert = jnp.argmax(onehot, axis=1).astype(jnp.int32)
    padded = ((counts + tm - 1) // tm) * tm
    ends = jnp.cumsum(padded)
    starts = ends - padded
    slot = (starts[expert] + rank).astype(jnp.int32)
    n_slots = TOP_K * rows + n_exp * tm
    token = jnp.tile(jnp.arange(rows, dtype=jnp.int32), TOP_K)
    slot_token = jnp.zeros((n_slots,), jnp.int32).at[slot].set(token)
    tile_start = jnp.arange(n_slots // tm, dtype=jnp.int32) * tm
    tile_valid = (tile_start < ends[-1]).astype(jnp.int32)
    tile_expert = jnp.minimum(jnp.searchsorted(ends, jnp.minimum(tile_start, ends[-1] - 1), side="right"),
                              n_exp - 1).astype(jnp.int32)

    xg = _gather_rows(h, slot_token)
    yg = _expert_mlp(xg, tile_expert, tile_valid, w1, w3, w2, tm)
    yc = _gather_rows(yg, slot)
    return _moe_out(yc, gates, code, xs, mods, ln_g, ln_b, dm, layer, rows)


def kernel(x, c, ctx, c_ctx, ada_w, ada_b, w_in, q_norm_g, k_norm_g, sgu_ln_g, sgu_ln_b, sgu_w, sgu_b,
           ret_decay_fwd, ret_decay_bwd, w_br_att, w_br_sgu, w_br_ret, w_out, ln1_g, ln1_b, ln2_g, ln2_b,
           ffn_w1, ffn_w3, ffn_w2, router_w, moe_w1, moe_w3, moe_w2):
    dm = _Dims(x, ctx, ada_w, q_norm_g, sgu_w, ret_decay_fwd, ffn_w1, moe_w1)
    assert dm.b + 1 <= MOD_ROWS and dm.ret_heads == dm.nq
    d = dm.d

    cond = jnp.concatenate([c, c_ctx[None], jnp.zeros((MOD_ROWS - dm.b - 1, d), F32)], axis=0)
    mods = _ada_mods(cond, ada_w, ada_b)
    rope = _rope_tables(dm)

    xs = jnp.concatenate([x.reshape(dm.n_lat, d), ctx.reshape(dm.n_ctx, d)], axis=0)
    for i in range(dm.depth):
        last = i == dm.depth - 1
        rows = dm.n_lat if last else dm.n_tok

        proj = _in_projection(xs, mods, _reorder_w_in(w_in[i], dm), dm, i)

        qp, kt, v_all = _attention_prep(proj, q_norm_g[i], k_norm_g[i], rope, dm)
        att = _attention(qp, kt, v_all, dm, with_ctx=not last)

        sgu = _sgu(proj, sgu_ln_g[i], sgu_ln_b[i], sgu_w[i], sgu_b[i], dm, rows)

        o_fwd = _retention_pass(proj, _ret_tables(ret_decay_fwd[i], dm, False), dm, backward=False)
        ret = _retention_pass(proj, _ret_tables(ret_decay_bwd[i], dm, True), dm, backward=True, o_fwd=o_fwd)

        xs1 = _merge(att, sgu, ret, proj, w_br_att[i].astype(BF16), w_br_sgu[i].astype(BF16),
                     w_br_ret[i].astype(BF16), w_out[i].astype(BF16), xs, mods, ln1_g[i], ln1_b[i], dm, i, rows)

        j = i // 2
        if i % 2 == 0:
            assert rows == xs1.shape[0]
            xs = _ffn(xs1, mods, ffn_w1[j].astype(BF16), ffn_w3[j].astype(BF16), ffn_w2[j].astype(BF16),
                      ln2_g[i], ln2_b[i], dm, i)
        else:
            xs = _moe(xs1, mods, router_w[j], moe_w1[j].astype(BF16), moe_w3[j].astype(BF16),
                      moe_w2[j].astype(BF16), ln2_g[i], ln2_b[i], dm, i, rows)
    return xs[:dm.n_lat].reshape(dm.b, dm.l, d)
```

```python
import functools
import math

import jax
import jax.numpy as jnp
from jax import lax
from jax.experimental import pallas as pl
from jax.experimental.pallas import tpu as pltpu

F32 = jnp.float32
BF16 = jnp.bfloat16
EPS = 1e-6
GRID_W = 64
ROPE_THETA = 10000.0
GQA_GROUP = 4
TOP_K = 2
N_MODS = 6
MOD_ROWS = 8
LANES = 128
V7X_VMEM_BYTES = 64 * 1024 * 1024
VMEM_LIMIT = V7X_VMEM_BYTES - 8 * 1024 * 1024
LOG2E = 1.4426950408889634
MAX_SAFE_SHIFT = 60.0


def _cparams(*sem):
    return pltpu.CompilerParams(dimension_semantics=sem, vmem_limit_bytes=VMEM_LIMIT)


def _resident(shape, index_map):
    return pl.BlockSpec(shape, index_map, pipeline_mode=pl.Buffered(1))


def _largest_divisor(n, candidates):
    for c in candidates:
        if n % c == 0:
            return c
    raise ValueError(f"no tile in {candidates} divides {n}")


def _ln(x):
    mu = jnp.mean(x, axis=-1, keepdims=True)
    xc = x - mu
    var = jnp.mean(xc * xc, axis=-1, keepdims=True)
    return xc * lax.rsqrt(var + EPS)


def _silu(x):
    return x * jax.nn.sigmoid(x)


def _gelu_tanh(x):
    return 0.5 * x * (1.0 + jnp.tanh(math.sqrt(2.0 / math.pi) * (x + 0.044715 * (x * x * x))))


def _store_token_major(ref, x):
    t, d = x.shape
    g = d // LANES
    for c in range(g):
        ref[pl.ds(c, t, stride=g), :] = x[:, c * LANES:(c + 1) * LANES]


def _load_token_major(ref, t, g):
    return jnp.concatenate([ref[pl.ds(c, t, stride=g), :] for c in range(g)], axis=1)


def _ada_kernel(c_ref, w_ref, b_ref, o_ref):
    s = _silu(c_ref[...]).astype(BF16)
    o_ref[...] = jnp.dot(s, w_ref[...].astype(BF16), preferred_element_type=F32) + b_ref[...]


def _ada_mods(cond, ada_w, ada_b):
    depth, d, nd = ada_w.shape
    tn = _largest_divisor(nd, (1024, 512, 256, 128))
    out = pl.pallas_call(
        _ada_kernel,
        grid=(depth, nd // tn),
        in_specs=[
            pl.BlockSpec((MOD_ROWS, d), lambda l, j: (0, 0)),
            pl.BlockSpec((None, d, tn), lambda l, j: (l, 0, j)),
            pl.BlockSpec((None, 1, tn), lambda l, j: (l, 0, j)),
        ],
        out_specs=pl.BlockSpec((None, MOD_ROWS, tn), lambda l, j: (l, 0, j)),
        out_shape=jax.ShapeDtypeStruct((depth, MOD_ROWS, nd), F32),
        compiler_params=_cparams("parallel", "parallel"),
        name="ada_mods",
    )(cond, ada_w, ada_b.reshape(depth, 1, nd))
    return out.reshape(depth * MOD_ROWS * N_MODS, 1, d)


class _Dims:
    def __init__(self, x, ctx, ada_w, q_norm_g, sgu_w, ret_decay_fwd, ffn_w1, moe_w1):
        self.b, self.l, self.d = x.shape
        self.ctx = ctx.shape[1]
        self.depth = ada_w.shape[0]
        self.hd = q_norm_g.shape[1]
        self.mw = self.d // 2
        self.nq = self.mw // self.hd
        self.nkv = self.nq // GQA_GROUP
        self.kvw = self.nkv * self.hd
        self.sgu_groups, self.sgu_chunk = sgu_w.shape[1], sgu_w.shape[2]
        self.ret_heads = ret_decay_fwd.shape[1]
        self.ret_chunk = 128
        self.n_lat = self.b * self.l
        self.n_ctx = self.b * self.ctx
        self.n_tok = self.n_lat + self.n_ctx
        self.lk = self.ctx + self.l
        self.n_exp = moe_w1.shape[1]
        self.alpha = (2 * self.depth) ** 0.25
        mw = self.mw
        self.c_u, self.c_v = 0, mw
        self.c_rq, self.c_rk, self.c_rv, self.c_rg = 2 * mw, 3 * mw, 4 * mw, 5 * mw
        self.c_bg = 6 * mw
        self.c_aq = 6 * mw + 3 * self.d
        self.c_ak = self.c_aq + mw
        self.c_av = self.c_ak + self.kvw
        self.n_cols = self.c_av + self.kvw

    def mod_row(self, layer, j):
        def f(i, tm):
            bidx = jnp.minimum((i * tm) // self.l, self.b)
            return (layer * MOD_ROWS + bidx) * N_MODS + j
        return f


def _reorder_w_in(w, dm):
    mw, kvw = dm.mw, dm.kvw
    aq = w[:, :mw]
    ak = w[:, mw:mw + kvw]
    av = w[:, mw + kvw:mw + 2 * kvw]
    rest = w[:, mw + 2 * kvw:]
    return jnp.concatenate([rest, aq, ak, av], axis=1).astype(BF16)


def _proj_kernel(x_ref, sh_ref, sc_ref, w_ref, o_ref, h_ref):
    @pl.when(pl.program_id(1) == 0)
    def _():
        h = _ln(x_ref[...]) * (1.0 + sc_ref[...]) + sh_ref[...]
        h_ref[...] = h.astype(BF16)

    o_ref[...] = jnp.dot(h_ref[...], w_ref[...], preferred_element_type=F32).astype(o_ref.dtype)


def _in_projection(xs, mods, w, dm, layer):
    t, d = xs.shape
    nc = w.shape[1]
    tm = _largest_divisor(math.gcd(dm.l, dm.n_ctx), (1024, 512, 256, 128))
    tn = _largest_divisor(nc, (dm.mw // 2,))
    row = dm.mod_row
    return pl.pallas_call(
        _proj_kernel,
        grid=(t // tm, nc // tn),
        in_specs=[
            pl.BlockSpec((tm, d), lambda i, j: (i, 0)),
            pl.BlockSpec((None, 1, d), lambda i, j: (row(layer, 0)(i, tm), 0, 0)),
            pl.BlockSpec((None, 1, d), lambda i, j: (row(layer, 1)(i, tm), 0, 0)),
            pl.BlockSpec((d, tn), lambda i, j: (0, j)),
        ],
        out_specs=pl.BlockSpec((tm, tn), lambda i, j: (i, j)),
        out_shape=jax.ShapeDtypeStruct((t, nc), BF16),
        scratch_shapes=[pltpu.VMEM((tm, d), BF16)],
        compiler_params=_cparams("parallel", "arbitrary"),
        name="in_projection",
    )(xs, mods, mods, w)


def _rope_tables(dm):
    n_freq = dm.hd // 4
    t = jnp.arange(dm.l, dtype=F32)
    row = jnp.floor(t / GRID_W)
    col = t - row * GRID_W
    inv_freq = ROPE_THETA ** (-jnp.arange(n_freq, dtype=F32) / n_freq)
    ar, ac = row[:, None] * inv_freq, col[:, None] * inv_freq
    z = jnp.zeros_like(ar)
    cos = jnp.concatenate([jnp.cos(ar), jnp.cos(ar), jnp.cos(ac), jnp.cos(ac)], axis=1)
    sin_lo = jnp.concatenate([z, jnp.sin(ar), z, jnp.sin(ac)], axis=1)
    sin_hi = jnp.concatenate([-jnp.sin(ar), z, -jnp.sin(ac), z], axis=1)
    pad = lambda a, v: jnp.concatenate([a, jnp.full((dm.ctx, dm.hd), v, F32)], axis=0)
    return pad(cos, 1.0), pad(sin_lo, 0.0), pad(sin_hi, 0.0)


def _prep_kernel(q_ref, k_ref, v_ref, qg_ref, kg_ref, cos_ref, slo_ref, shi_ref, qo_ref, kt_ref, vo_ref,
                 *, nq, nkv, hd, qscale):
    cos, slo, shi = cos_ref[...], slo_ref[...], shi_ref[...]
    quarter = hd // 4

    def norm_rope(xh, g):
        y = xh * lax.rsqrt(jnp.mean(xh * xh, axis=-1, keepdims=True) + EPS) * g
        return y * cos + pltpu.roll(y, quarter, 1) * slo + pltpu.roll(y, hd - quarter, 1) * shi

    qg = qg_ref[...] * qscale
    kg = kg_ref[...]
    for h in range(nq):
        sl = slice(h * hd, (h + 1) * hd)
        qo_ref[:, sl] = norm_rope(q_ref[:, sl].astype(F32), qg).astype(qo_ref.dtype)
    for g in range(nkv):
        sl = slice(g * hd, (g + 1) * hd)
        kt_ref[g] = norm_rope(k_ref[:, sl].astype(F32), kg).T.astype(kt_ref.dtype)
        vo_ref[g] = v_ref[:, sl]


def _attention_prep(proj, q_norm_g, k_norm_g, tables, dm):
    t = proj.shape[0]
    tp = _largest_divisor(math.gcd(dm.l, dm.ctx), (256, 128))
    mw, kvw, hd = dm.mw, dm.kvw, dm.hd
    n_lat_tiles = dm.n_lat // tp

    def batch_of(i):
        return jnp.where(i < n_lat_tiles, (i * tp) // dm.l, (i * tp - dm.n_lat) // dm.ctx)

    def key_block(i):
        lat = (dm.ctx + (i * tp) % dm.l) // tp
        ctx = ((i * tp - dm.n_lat) % dm.ctx) // tp
        return jnp.where(i < n_lat_tiles, lat, ctx)

    def table_block(i):
        lat = ((i * tp) % dm.l) // tp
        ctx = dm.l // tp + ((i * tp - dm.n_lat) % dm.ctx) // tp
        return jnp.where(i < n_lat_tiles, lat, ctx)

    tab_spec = pl.BlockSpec((tp, hd), lambda i: (table_block(i), 0))
    qscale = hd ** -0.5 * LOG2E
    return pl.pallas_call(
        functools.partial(_prep_kernel, nq=dm.nq, nkv=dm.nkv, hd=hd, qscale=qscale),
        grid=(t // tp,),
        in_specs=[
            pl.BlockSpec((tp, mw), lambda i: (i, dm.c_aq // mw)),
            pl.BlockSpec((tp, kvw), lambda i: (i, dm.c_ak // kvw)),
            pl.BlockSpec((tp, kvw), lambda i: (i, dm.c_av // kvw)),
            pl.BlockSpec((1, hd), lambda i: (0, 0)),
            pl.BlockSpec((1, hd), lambda i: (0, 0)),
            tab_spec, tab_spec, tab_spec,
        ],
        out_specs=[
            pl.BlockSpec((tp, mw), lambda i: (i, 0)),
            pl.BlockSpec((None, dm.nkv, hd, tp), lambda i: (batch_of(i), 0, 0, key_block(i))),
            pl.BlockSpec((None, dm.nkv, tp, hd), lambda i: (batch_of(i), 0, key_block(i), 0)),
        ],
        out_shape=[
            jax.ShapeDtypeStruct((t, mw), BF16),
            jax.ShapeDtypeStruct((dm.b, dm.nkv, hd, dm.lk), BF16),
            jax.ShapeDtypeStruct((dm.b, dm.nkv, dm.lk, hd), BF16),
        ],
        compiler_params=_cparams("parallel"),
        name="attention_prep",
    )(proj, proj, proj, q_norm_g.reshape(1, hd), k_norm_g.reshape(1, hd), *tables)


def _attn_kernel(q_ref, kt_ref, v_ref, o_ref, m_ref, l_ref, acc_ref, kmax_ref, *,
                 hd, n_lat_tiles, lat_keys, ctx_keys):
    tq = q_ref.shape[0]
    i = pl.program_id(2)
    q = q_ref[...]
    qs = jnp.concatenate([q[:, h * hd:(h + 1) * hd] for h in range(GQA_GROUP)], axis=0)

    @pl.when(i == 0)
    def _():
        k = kt_ref[...].astype(F32)
        k2 = jnp.max(jnp.sum(k * k, axis=0, keepdims=True), axis=1, keepdims=True)
        kmax_ref[...] = jnp.broadcast_to(jnp.sqrt(k2), kmax_ref.shape)

    qf = qs.astype(F32)
    shift = jnp.sqrt(jnp.sum(qf * qf, axis=-1, keepdims=True)) * kmax_ref[0:1, 0:1]
    bounded = jnp.max(shift) <= MAX_SAFE_SHIFT

    def finish(l):
        o = acc_ref[...] * (1.0 / l)
        o_ref[...] = jnp.concatenate([o[h * tq:(h + 1) * tq] for h in range(GQA_GROUP)],
                                     axis=1).astype(o_ref.dtype)

    def attend_bounded(tk, nk):
        cb = _largest_divisor(tk, (256, 128))
        m_ref[...] = jnp.broadcast_to(shift, m_ref.shape)
        l_ref[...] = jnp.zeros(l_ref.shape, F32)
        acc_ref[...] = jnp.zeros(acc_ref.shape, F32)

        def body(kt, carry):
            off = kt * tk
            mm = m_ref[...]
            mm = jnp.concatenate([mm] * (cb // 128), axis=1)
            lsum = l_ref[...]
            parts = []
            for c in range(tk // cb):
                col = pl.multiple_of(off + c * cb, cb)
                p = jnp.exp2(jnp.dot(qs, kt_ref[:, pl.ds(col, cb)], preferred_element_type=F32) - mm)
                for u in range(cb // 128):
                    lsum = lsum + p[:, u * 128:(u + 1) * 128]
                parts.append(p.astype(BF16))
            l_ref[...] = lsum
            acc_ref[...] += jnp.dot(jnp.concatenate(parts, axis=1), v_ref[pl.ds(pl.multiple_of(off, tk), tk), :],
                                    preferred_element_type=F32)
            return carry

        lax.fori_loop(0, nk, body, 0)
        finish(jnp.sum(l_ref[...], axis=-1, keepdims=True))

    def attend_online(tk, nk):
        m_ref[...] = jnp.full(m_ref.shape, -jnp.inf, F32)
        l_ref[...] = jnp.zeros(l_ref.shape, F32)
        acc_ref[...] = jnp.zeros(acc_ref.shape, F32)

        def body(kt, carry):
            off = pl.multiple_of(kt * tk, tk)
            s = jnp.dot(qs, kt_ref[:, pl.ds(off, tk)], preferred_element_type=F32)
            m_prev = m_ref[:, 0:1]
            m_new = jnp.maximum(m_prev, jnp.max(s, axis=-1, keepdims=True))
            p = jnp.exp2(s - m_new)
            a = jnp.exp2(m_prev - m_new)
            l_ref[:, 0:1] = a * l_ref[:, 0:1] + jnp.sum(p, axis=-1, keepdims=True)
            acc_ref[...] = a * acc_ref[...] + jnp.dot(p.astype(BF16), v_ref[pl.ds(off, tk), :],
                                                      preferred_element_type=F32)
            m_ref[:, 0:1] = m_new
            return carry

        lax.fori_loop(0, nk, body, 0)
        finish(l_ref[:, 0:1])

    def attend(keys):
        @pl.when(bounded)
        def _():
            attend_bounded(*keys)

        @pl.when(jnp.logical_not(bounded))
        def _():
            attend_online(*keys)

    is_latent = i < n_lat_tiles

    @pl.when(is_latent)
    def _():
        attend(lat_keys)

    if ctx_keys is not None:
        @pl.when(jnp.logical_not(is_latent))
        def _():
            attend(ctx_keys)


def _attention(qp, kt, v, dm, *, with_ctx):
    hd, mw = dm.hd, dm.mw
    tq = _largest_divisor(math.gcd(dm.l, dm.ctx), (256, 128))
    key_tiling = lambda n: (lambda tk: (tk, n // tk))(_largest_divisor(n, (768, 512, 256, 128)))
    gw = GQA_GROUP * hd
    nl, nc = dm.l // tq, dm.ctx // tq
    rows = dm.n_tok if with_ctx else dm.n_lat

    def q_map(b, g, i):
        return jnp.where(i < nl, b * nl + i, dm.n_lat // tq + b * nc + (i - nl)), g

    return pl.pallas_call(
        functools.partial(_attn_kernel, hd=hd, n_lat_tiles=nl, lat_keys=key_tiling(dm.lk),
                          ctx_keys=key_tiling(dm.ctx) if with_ctx else None),
        grid=(dm.b, dm.nkv, nl + (nc if with_ctx else 0)),
        in_specs=[
            pl.BlockSpec((tq, gw), q_map),
            pl.BlockSpec((None, None, hd, dm.lk), lambda b, g, i: (b, g, 0, 0)),
            pl.BlockSpec((None, None, dm.lk, hd), lambda b, g, i: (b, g, 0, 0)),
        ],
        out_specs=pl.BlockSpec((tq, gw), q_map),
        out_shape=jax.ShapeDtypeStruct((rows, mw), BF16),
        scratch_shapes=[
            pltpu.VMEM((GQA_GROUP * tq, 128), F32),
            pltpu.VMEM((GQA_GROUP * tq, 128), F32),
            pltpu.VMEM((GQA_GROUP * tq, hd), F32),
            pltpu.VMEM((8, 128), F32),
        ],
        compiler_params=_cparams("parallel", "parallel", "arbitrary"),
        name="attention",
    )(qp, kt, v)


def _sgu_kernel(u_ref, v_ref, g_ref, b_ref, w_ref, bs_ref, o_ref, *, ng, gw, chunk):
    ts = u_ref.shape[0]
    u = _gelu_tanh(u_ref[...].astype(F32))
    v = _ln(_gelu_tanh(v_ref[...].astype(F32))) * g_ref[...] + b_ref[...]
    vb = v.astype(BF16)
    for c in range(ts // chunk):
        rows = slice(c * chunk, (c + 1) * chunk)
        for g in range(ng):
            cols = slice(g * gw, (g + 1) * gw)
            z = jnp.dot(w_ref[g], vb[rows, cols], preferred_element_type=F32) + bs_ref[g]
            o_ref[rows, cols] = (u[rows, cols] * z).astype(o_ref.dtype)


def _sgu(proj, ln_g, ln_b, w_s, b_s, dm, rows):
    mw, chunk, ng = dm.mw, dm.sgu_chunk, dm.sgu_groups
    ts = _largest_divisor(math.gcd(dm.l, dm.n_ctx), (256, 128))
    return pl.pallas_call(
        functools.partial(_sgu_kernel, ng=ng, gw=mw // ng, chunk=chunk),
        grid=(rows // ts,),
        in_specs=[
            pl.BlockSpec((ts, mw), lambda i: (i, dm.c_u // mw)),
            pl.BlockSpec((ts, mw), lambda i: (i, dm.c_v // mw)),
            pl.BlockSpec((1, mw), lambda i: (0, 0)),
            pl.BlockSpec((1, mw), lambda i: (0, 0)),
            pl.BlockSpec((ng, chunk, chunk), lambda i: (0, 0, 0)),
            pl.BlockSpec((ng, chunk, 1), lambda i: (0, 0, 0)),
        ],
        out_specs=pl.BlockSpec((ts, mw), lambda i: (i, 0)),
        out_shape=jax.ShapeDtypeStruct((rows, mw), BF16),
        compiler_params=_cparams("parallel"),
        name="sgu",
    )(proj, proj, ln_g.reshape(1, mw), ln_b.reshape(1, mw), w_s.astype(BF16), b_s.reshape(ng, chunk, 1))


def _ret_tables(decay_logit, dm, backward):
    c, hd, nh = dm.ret_chunk, dm.hd, dm.ret_heads
    lg = jax.nn.log_sigmoid(decay_logit.astype(F32))
    pos = jnp.arange(c, dtype=F32)
    diff = pos[:, None] - pos[None, :]
    if backward:
        diff = -diff
        xi_e = c - pos
        zeta_e = pos
    else:
        xi_e = pos + 1.0
        zeta_e = c - 1.0 - pos
    dmat = jnp.where(diff >= 0, jnp.exp(jnp.maximum(diff, 0.0)[None] * lg[:, None, None]), 0.0)
    dmat = dmat * hd ** -0.5
    xi = jnp.exp(xi_e[None, :] * lg[:, None])[..., None]
    zeta = (jnp.exp(zeta_e[None, :] * lg[:, None]) * hd ** -0.5)[..., None]
    dec = jnp.broadcast_to(jnp.exp(c * lg)[:, None, None], (nh, 1, hd))
    return dmat.astype(F32), xi.astype(F32), zeta.astype(F32), dec.astype(F32)


def _ret_kernel(*refs, nh, hd, final):
    if final:
        q_ref, k_ref, v_ref, dm_ref, xi_ref, zeta_ref, dec_ref, of_ref, g_ref, o_ref, st_ref = refs
    else:
        q_ref, k_ref, v_ref, dm_ref, xi_ref, zeta_ref, dec_ref, o_ref, st_ref = refs

    @pl.when(pl.program_id(1) == 0)
    def _():
        st_ref[...] = jnp.zeros(st_ref.shape, F32)

    for h in range(nh):
        sl = slice(h * hd, (h + 1) * hd)
        qh, kh, vh = q_ref[:, sl], k_ref[:, sl], v_ref[:, sl]
        s = lax.dot_general(qh, kh, (((1,), (1,)), ((), ())), preferred_element_type=F32)
        inner = (s * dm_ref[h]).astype(BF16)
        st = st_ref[h]
        o = jnp.dot(inner, vh, preferred_element_type=F32)
        o = o + jnp.dot(qh, st.astype(BF16), preferred_element_type=F32) * xi_ref[h]
        kz = (kh.astype(F32) * zeta_ref[h]).astype(BF16)
        st_ref[h] = st * dec_ref[h] + lax.dot_general(kz, vh, (((0,), (0,)), ((), ())),
                                                      preferred_element_type=F32)
        if final:
            gn = _ln(of_ref[:, sl] + o)
            o_ref[:, sl] = (_silu(g_ref[:, sl].astype(F32)) * gn).astype(o_ref.dtype)
        else:
            o_ref[:, sl] = o


def _retention_pass(proj, tables, dm, *, backward, o_fwd=None):
    c, mw, nh, hd = dm.ret_chunk, dm.mw, dm.ret_heads, dm.hd
    n_cc, n_lc = dm.ctx // c, dm.l // c
    steps = n_cc + n_lc

    def row_block(b, t):
        if backward:
            ctx = dm.n_lat // c + b * n_cc + (n_cc - 1 - t)
            lat = b * n_lc + (n_lc - 1 - (t - n_cc))
        else:
            ctx = dm.n_lat // c + b * n_cc + t
            lat = b * n_lc + (t - n_cc)
        return jnp.where(t < n_cc, ctx, lat)

    col = lambda off: (lambda b, t: (row_block(b, t), off // mw))
    const3 = lambda b, t: (0, 0, 0)
    in_specs = [
        pl.BlockSpec((c, mw), col(dm.c_rq)),
        pl.BlockSpec((c, mw), col(dm.c_rk)),
        pl.BlockSpec((c, mw), col(dm.c_rv)),
        pl.BlockSpec((nh, c, c), const3),
        pl.BlockSpec((nh, c, 1), const3),
        pl.BlockSpec((nh, c, 1), const3),
        pl.BlockSpec((nh, 1, hd), const3),
    ]
    args = [proj, proj, proj, *tables]
    final = o_fwd is not None
    if final:
        in_specs += [pl.BlockSpec((c, mw), col(0)), pl.BlockSpec((c, mw), col(dm.c_rg))]
        args += [o_fwd, proj]
    return pl.pallas_call(
        functools.partial(_ret_kernel, nh=nh, hd=hd, final=final),
        grid=(dm.b, steps),
        in_specs=in_specs,
        out_specs=pl.BlockSpec((c, mw), col(0)),
        out_shape=jax.ShapeDtypeStruct((dm.n_tok, mw), BF16 if final else F32),
        scratch_shapes=[pltpu.VMEM((nh, hd, hd), F32)],
        compiler_params=_cparams("parallel", "arbitrary"),
        name="retention_bwd" if final else "retention_fwd",
    )(*args)


def _merge_kernel(att_ref, sgu_ref, ret_ref, ga_ref, gs_ref, gr_ref, wa_ref, ws_ref, wr_ref, wo_ref,
                  x_ref, g1_ref, lng_ref, lnb_ref, o_ref, *, alpha):
    def branch(a_ref, w_ref, gate_ref):
        return jax.nn.sigmoid(gate_ref[...].astype(F32)) * jnp.dot(a_ref[...], w_ref[...],
                                                                    preferred_element_type=F32)

    m = branch(att_ref, wa_ref, ga_ref) + branch(sgu_ref, ws_ref, gs_ref) + branch(ret_ref, wr_ref, gr_ref)
    mix = jnp.dot(m.astype(BF16), wo_ref[...], preferred_element_type=F32)
    y = alpha * x_ref[...] + g1_ref[...] * mix
    o_ref[...] = _ln(y) * lng_ref[...] + lnb_ref[...]


def _merge(att, sgu, ret, proj, w_att, w_sgu, w_ret, w_o, xs, mods, ln_g, ln_b, dm, layer, rows):
    d, mw = dm.d, dm.mw
    tm = _largest_divisor(math.gcd(dm.l, dm.n_ctx), (256, 128))
    row = dm.mod_row
    tok = lambda w: pl.BlockSpec((tm, w), lambda i: (i, 0))
    gate = lambda k: pl.BlockSpec((tm, d), lambda i: (i, dm.c_bg // d + k))
    vec = pl.BlockSpec((1, d), lambda i: (0, 0))
    return pl.pallas_call(
        functools.partial(_merge_kernel, alpha=dm.alpha),
        grid=(rows // tm,),
        in_specs=[
            tok(mw), tok(mw), tok(mw), gate(0), gate(1), gate(2),
            _resident((mw, d), lambda i: (0, 0)), _resident((mw, d), lambda i: (0, 0)),
            _resident((mw, d), lambda i: (0, 0)), _resident((d, d), lambda i: (0, 0)),
            tok(d),
            pl.BlockSpec((None, 1, d), lambda i: (row(layer, 2)(i, tm), 0, 0)),
            vec, vec,
        ],
        out_specs=tok(d),
        out_shape=jax.ShapeDtypeStruct((rows, d), F32),
        compiler_params=_cparams("parallel"),
        name="merge",
    )(att, sgu, ret, proj, proj, proj, w_att, w_sgu, w_ret, w_o, xs, mods,
      ln_g.reshape(1, d), ln_b.reshape(1, d))


def _ffn_kernel(x_ref, sh_ref, sc_ref, w1_ref, w3_ref, w2_ref, g2_ref, lng_ref, lnb_ref, o_ref,
                h_ref, acc_ref, *, alpha):
    j = pl.program_id(1)

    @pl.when(j == 0)
    def _():
        h = _ln(x_ref[...]) * (1.0 + sc_ref[...]) + sh_ref[...]
        h_ref[...] = h.astype(BF16)
        acc_ref[...] = jnp.zeros(acc_ref.shape, F32)

    h = h_ref[...]
    a = jnp.dot(h, w1_ref[...], preferred_element_type=F32)
    b = jnp.dot(h, w3_ref[...], preferred_element_type=F32)
    acc_ref[...] += jnp.dot((_silu(a) * b).astype(BF16), w2_ref[...], preferred_element_type=F32)

    @pl.when(j == pl.num_programs(1) - 1)
    def _():
        y = alpha * x_ref[...] + g2_ref[...] * acc_ref[...]
        o_ref[...] = _ln(y) * lng_ref[...] + lnb_ref[...]


def _ffn(xs, mods, w1, w3, w2, ln_g, ln_b, dm, layer):
    t, d = xs.shape
    f = w1.shape[1]
    tm = _largest_divisor(math.gcd(dm.l, dm.n_ctx), (512, 256, 128))
    tf = _largest_divisor(f, (512, 256, 128))
    row = dm.mod_row
    mod = lambda k: pl.BlockSpec((None, 1, d), lambda i, j: (row(layer, k)(i, tm), 0, 0))
    vec = pl.BlockSpec((1, d), lambda i, j: (0, 0))
    return pl.pallas_call(
        functools.partial(_ffn_kernel, alpha=dm.alpha),
        grid=(t // tm, f // tf),
        in_specs=[
            pl.BlockSpec((tm, d), lambda i, j: (i, 0)),
            mod(3), mod(4),
            pl.BlockSpec((d, tf), lambda i, j: (0, j)),
            pl.BlockSpec((d, tf), lambda i, j: (0, j)),
            pl.BlockSpec((tf, d), lambda i, j: (j, 0)),
            mod(5), vec, vec,
        ],
        out_specs=pl.BlockSpec((tm, d), lambda i, j: (i, 0)),
        out_shape=jax.ShapeDtypeStruct((t, d), F32),
        scratch_shapes=[pltpu.VMEM((tm, d), BF16), pltpu.VMEM((tm, d), F32)],
        compiler_params=_cparams("parallel", "arbitrary"),
        name="ffn_dense",
    )(xs, mods, mods, w1, w3, w2, mods, ln_g.reshape(1, d), ln_b.reshape(1, d))


def _router_kernel(x_ref, sh_ref, sc_ref, whi_ref, wlo_ref, h_ref, gates_ref, code_ref):
    h = _ln(x_ref[...]) * (1.0 + sc_ref[...]) + sh_ref[...]
    _store_token_major(h_ref, h)
    h_hi = h.astype(BF16)
    h_lo = (h - h_hi.astype(F32)).astype(BF16)
    w_hi = whi_ref[...]
    logits = (jnp.dot(h_hi, w_hi, preferred_element_type=F32)
              + jnp.dot(h_lo, w_hi, preferred_element_type=F32)
              + jnp.dot(h_hi, wlo_ref[...], preferred_element_type=F32))
    n_exp = float(logits.shape[-1])
    ids = lax.broadcasted_iota(jnp.int32, logits.shape, 1).astype(F32)
    m1 = jnp.max(logits, axis=-1, keepdims=True)
    i1 = jnp.min(jnp.where(logits == m1, ids, n_exp), axis=-1, keepdims=True)
    rest = jnp.where(ids == i1, -jnp.inf, logits)
    m2 = jnp.max(rest, axis=-1, keepdims=True)
    i2 = jnp.min(jnp.where(rest == m2, ids, n_exp), axis=-1, keepdims=True)
    e = jnp.exp(m2 - m1)
    w1 = 1.0 / (1.0 + e)
    w2 = e * w1
    gates_ref[...] = jnp.where(ids == i1, w1, jnp.where(ids == i2, w2, 0.0))
    code_ref[...] = jnp.where(ids == i1, 1, jnp.where(ids == i2, 2, 0)).astype(jnp.int32)


def _router(xs, mods, router_w, dm, layer, rows):
    d, n_exp = router_w.shape
    tm = _largest_divisor(dm.l, (512, 256, 128))
    row = dm.mod_row
    mod = lambda k: pl.BlockSpec((None, 1, d), lambda i: (row(layer, k)(i, tm), 0, 0))
    w_hi = router_w.astype(BF16)
    w_lo = (router_w - w_hi.astype(F32)).astype(BF16)
    return pl.pallas_call(
        _router_kernel,
        grid=(rows // tm,),
        in_specs=[
            pl.BlockSpec((tm, d), lambda i: (i, 0)), mod(3), mod(4),
            pl.BlockSpec((d, n_exp), lambda i: (0, 0)),
            pl.BlockSpec((d, n_exp), lambda i: (0, 0)),
        ],
        out_specs=[
            pl.BlockSpec((tm * (d // LANES), LANES), lambda i: (i, 0)),
            pl.BlockSpec((tm, n_exp), lambda i: (i, 0)),
            pl.BlockSpec((tm, n_exp), lambda i: (i, 0)),
        ],
        out_shape=[
            jax.ShapeDtypeStruct((rows * (d // LANES), LANES), F32),
            jax.ShapeDtypeStruct((rows, n_exp), F32),
            jax.ShapeDtypeStruct((rows, n_exp), jnp.int32),
        ],
        compiler_params=_cparams("parallel"),
        name="router",
    )(xs, mods, mods, w_hi, w_lo)


def _row_gather(index_of, n, src_ref, dst_ref, sem, g, *, wait):
    def body(r, carry):
        cp = pltpu.make_async_copy(src_ref.at[pl.ds(pl.multiple_of(index_of(r) * g, g), g), :],
                                   dst_ref.at[pl.ds(pl.multiple_of(r * g, g), g), :], sem)
        if wait:
            cp.wait()
        else:
            cp.start()
        return carry

    lax.fori_loop(0, n, body, 0)


def _expert_kernel(te_ref, tv_ref, cur_ref, nxt_ref, h_hbm, w1_ref, w3_ref, w2_ref, o_ref,
                   buf_ref, sem, h_ref, acc_ref):
    i, j = pl.program_id(0), pl.program_id(1)
    nt = pl.num_programs(0)
    tm = h_ref.shape[0]
    g = h_ref.shape[1] // LANES
    slot = i % 2
    gather = functools.partial(_row_gather, n=tm, src_ref=h_hbm, g=g)

    @pl.when(tv_ref[i] > 0)
    def _():
        @pl.when(j == 0)
        def _():
            @pl.when(i == 0)
            def _():
                gather(lambda r: cur_ref[0, 0, r], dst_ref=buf_ref.at[0], sem=sem.at[0], wait=False)

            gather(lambda r: cur_ref[0, 0, r], dst_ref=buf_ref.at[slot], sem=sem.at[slot], wait=True)
            h_ref[...] = _load_token_major(buf_ref.at[slot], tm, g).astype(BF16)
            acc_ref[...] = jnp.zeros(acc_ref.shape, F32)

            @pl.when(jnp.logical_and(i + 1 < nt, tv_ref[jnp.minimum(i + 1, nt - 1)] > 0))
            def _():
                gather(lambda r: nxt_ref[0, 0, r], dst_ref=buf_ref.at[1 - slot], sem=sem.at[1 - slot], wait=False)

        h = h_ref[...]
        a = jnp.dot(h, w1_ref[...], preferred_element_type=F32)
        b = jnp.dot(h, w3_ref[...], preferred_element_type=F32)
        acc_ref[...] += jnp.dot((_silu(a) * b).astype(BF16), w2_ref[...], preferred_element_type=F32)

        @pl.when(j == pl.num_programs(1) - 1)
        def _():
            _store_token_major(o_ref, acc_ref[...])

    @pl.when(jnp.logical_and(tv_ref[i] == 0, j == 0))
    def _():
        o_ref[...] = jnp.zeros(o_ref.shape, o_ref.dtype)


def _expert_mlp(h_tm, slot_token, tile_expert, tile_valid, w1, w3, w2, tm, d):
    n_slots = slot_token.shape[0]
    f = w1.shape[2]
    g = d // LANES
    tf = _largest_divisor(f, (512, 256, 128))
    nj, nt = f // tf, n_slots // tm
    jj = lambda i, j, tv: jnp.where(tv[i] > 0, j, nj - 1)
    idx = slot_token.reshape(nt, 1, tm)
    grid_spec = pltpu.PrefetchScalarGridSpec(
        num_scalar_prefetch=2,
        grid=(nt, nj),
        in_specs=[
            pl.BlockSpec((1, 1, tm), lambda i, j, te, tv: (i, 0, 0), memory_space=pltpu.SMEM),
            pl.BlockSpec((1, 1, tm), lambda i, j, te, tv: (jnp.minimum(i + 1, nt - 1), 0, 0),
                         memory_space=pltpu.SMEM),
            pl.BlockSpec(memory_space=pl.ANY),
            pl.BlockSpec((None, d, tf), lambda i, j, te, tv: (te[i], 0, jj(i, j, tv))),
            pl.BlockSpec((None, d, tf), lambda i, j, te, tv: (te[i], 0, jj(i, j, tv))),
            pl.BlockSpec((None, tf, d), lambda i, j, te, tv: (te[i], jj(i, j, tv), 0)),
        ],
        out_specs=pl.BlockSpec((tm * g, LANES), lambda i, j, te, tv: (i, 0)),
        scratch_shapes=[
            pltpu.VMEM((2, tm * g, LANES), F32),
            pltpu.SemaphoreType.DMA((2,)),
            pltpu.VMEM((tm, d), BF16),
            pltpu.VMEM((tm, d), F32),
        ],
    )
    return pl.pallas_call(
        _expert_kernel,
        grid_spec=grid_spec,
        out_shape=jax.ShapeDtypeStruct((n_slots * g, LANES), F32),
        compiler_params=_cparams("arbitrary", "arbitrary"),
        name="expert_mlp",
    )(tile_expert, tile_valid, idx, idx, h_tm, w1, w3, w2)


def _moe_out_kernel(cur_ref, nxt_ref, y_hbm, gates_ref, code_ref, x_ref, g2_ref, lng_ref, lnb_ref, o_ref,
                    buf_ref, sem, *, alpha):
    i, nt = pl.program_id(0), pl.num_programs(0)
    tm, d = x_ref.shape
    g = d // LANES
    slot = i % 2

    def gather(idx_ref, s, wait):
        for k in range(TOP_K):
            _row_gather(lambda r: idx_ref[0, k, r], tm, y_hbm, buf_ref.at[s, k], sem.at[s], g, wait=wait)

    @pl.when(i == 0)
    def _():
        gather(cur_ref, 0, False)

    gather(cur_ref, slot, True)

    @pl.when(i + 1 < nt)
    def _():
        gather(nxt_ref, 1 - slot, False)

    gates, code = gates_ref[...], code_ref[...]
    f = jnp.zeros((tm, d), F32)
    for k in range(TOP_K):
        w = jnp.sum(jnp.where(code == k + 1, gates, 0.0), axis=-1, keepdims=True)
        f = f + w * _load_token_major(buf_ref.at[slot, k], tm, g)
    y = alpha * x_ref[...] + g2_ref[...] * f
    o_ref[...] = _ln(y) * lng_ref[...] + lnb_ref[...]


def _moe_out(y_tm, slot, gates, code, xs, mods, ln_g, ln_b, dm, layer, rows):
    d, n_exp = dm.d, gates.shape[1]
    g = d // LANES
    tm = _largest_divisor(dm.l, (256, 128))
    nt = rows // tm
    row = dm.mod_row
    vec = pl.BlockSpec((1, d), lambda i: (0, 0))
    idx = slot.reshape(TOP_K, nt, tm).transpose(1, 0, 2)
    return pl.pallas_call(
        functools.partial(_moe_out_kernel, alpha=dm.alpha),
        grid=(nt,),
        in_specs=[
            pl.BlockSpec((1, TOP_K, tm), lambda i: (i, 0, 0), memory_space=pltpu.SMEM),
            pl.BlockSpec((1, TOP_K, tm), lambda i: (jnp.minimum(i + 1, nt - 1), 0, 0), memory_space=pltpu.SMEM),
            pl.BlockSpec(memory_space=pl.ANY),
            pl.BlockSpec((tm, n_exp), lambda i: (i, 0)),
            pl.BlockSpec((tm, n_exp), lambda i: (i, 0)),
            pl.BlockSpec((tm, d), lambda i: (i, 0)),
            pl.BlockSpec((None, 1, d), lambda i: (row(layer, 5)(i, tm), 0, 0)),
            vec, vec,
        ],
        out_specs=pl.BlockSpec((tm, d), lambda i: (i, 0)),
        out_shape=jax.ShapeDtypeStruct((rows, d), F32),
        scratch_shapes=[pltpu.VMEM((2, TOP_K, tm * g, LANES), F32), pltpu.SemaphoreType.DMA((2,))],
        compiler_params=_cparams("arbitrary"),
        name="moe_out",
    )(idx, idx, y_tm, gates, code, xs, mods, ln_g.reshape(1, d), ln_b.reshape(1, d))


def _moe(xs, mods, router_w, w1, w3, w2, ln_g, ln_b, dm, layer, rows):
    n_exp = router_w.shape[1]
    tm = 512 if rows * TOP_K >= 512 * n_exp * 4 else 128
    h, gates, code = _router(xs, mods, router_w, dm, layer, rows)

    onehot = jnp.concatenate([code == 1, code == 2], axis=0).astype(jnp.int32)
    csum = jnp.cumsum(onehot, axis=0)
    counts = csum[-1]
    rank = jnp.sum(csum * onehot, axis=1) - 1
    padded = ((counts + tm - 1) // tm) * tm
    ends = jnp.cumsum(padded)
    starts = ends - padded
    slot = (jnp.sum(onehot * starts[None, :], axis=1) + rank).astype(jnp.int32)
    n_slots = TOP_K * rows + n_exp * tm
    token = jnp.tile(jnp.arange(rows, dtype=jnp.int32), TOP_K)
    slot_token = jnp.zeros((n_slots,), jnp.int32).at[slot].set(token)
    tile_start = jnp.arange(n_slots // tm, dtype=jnp.int32) * tm
    tile_valid = (tile_start < ends[-1]).astype(jnp.int32)
    last_used = jnp.minimum(tile_start, ends[-1] - 1)
    tile_expert = jnp.minimum(jnp.sum((ends[None, :] <= last_used[:, None]).astype(jnp.int32), axis=1),
                              n_exp - 1).astype(jnp.int32)

    y = _expert_mlp(h, slot_token, tile_expert, tile_valid, w1, w3, w2, tm, dm.d)
    return _moe_out(y, slot, gates, code, xs, mods, ln_g, ln_b, dm, layer, rows)


def kernel(x, c, ctx, c_ctx, ada_w, ada_b, w_in, q_norm_g, k_norm_g, sgu_ln_g, sgu_ln_b, sgu_w, sgu_b,
           ret_decay_fwd, ret_decay_bwd, w_br_att, w_br_sgu, w_br_ret, w_out, ln1_g, ln1_b, ln2_g, ln2_b,
           ffn_w1, ffn_w3, ffn_w2, router_w, moe_w1, moe_w3, moe_w2):
    dm = _Dims(x, ctx, ada_w, q_norm_g, sgu_w, ret_decay_fwd, ffn_w1, moe_w1)
    assert dm.b + 1 <= MOD_ROWS and dm.ret_heads == dm.nq
    d = dm.d

    cond = jnp.concatenate([c, c_ctx[None], jnp.zeros((MOD_ROWS - dm.b - 1, d), F32)], axis=0)
    mods = _ada_mods(cond, ada_w, ada_b)
    rope = _rope_tables(dm)

    xs = jnp.concatenate([x.reshape(dm.n_lat, d), ctx.reshape(dm.n_ctx, d)], axis=0)
    for i in range(dm.depth):
        last = i == dm.depth - 1
        rows = dm.n_lat if last else dm.n_tok

        proj = _in_projection(xs, mods, _reorder_w_in(w_in[i], dm), dm, i)

        qp, kt, v_all = _attention_prep(proj, q_norm_g[i], k_norm_g[i], rope, dm)
        att = _attention(qp, kt, v_all, dm, with_ctx=not last)

        sgu = _sgu(proj, sgu_ln_g[i], sgu_ln_b[i], sgu_w[i], sgu_b[i], dm, rows)

        o_fwd = _retention_pass(proj, _ret_tables(ret_decay_fwd[i], dm, False), dm, backward=False)
        ret = _retention_pass(proj, _ret_tables(ret_decay_bwd[i], dm, True), dm, backward=True, o_fwd=o_fwd)

        xs1 = _merge(att, sgu, ret, proj, w_br_att[i].astype(BF16), w_br_sgu[i].astype(BF16),
                     w_br_ret[i].astype(BF16), w_out[i].astype(BF16), xs, mods, ln1_g[i], ln1_b[i], dm, i, rows)

        j = i // 2
        if i % 2 == 0:
            assert rows == xs1.shape[0]
            xs = _ffn(xs1, mods, ffn_w1[j].astype(BF16), ffn_w3[j].astype(BF16), ffn_w2[j].astype(BF16),
                      ln2_g[i], ln2_b[i], dm, i)
        else:
            xs = _moe(xs1, mods, router_w[j], moe_w1[j].astype(BF16), moe_w3[j].astype(BF16),
                      moe_w2[j].astype(BF16), ln2_g[i], ln2_b[i], dm, i, rows)
    return xs[:dm.n_lat].reshape(dm.b, dm.l, d)
```

```python
import functools
import math

import jax
import jax.numpy as jnp
from jax import lax
from jax.experimental import pallas as pl
from jax.experimental.pallas import tpu as pltpu

F32 = jnp.float32
BF16 = jnp.bfloat16
EPS = 1e-6
GRID_W = 64
ROPE_THETA = 10000.0
GQA_GROUP = 4
TOP_K = 2
N_MODS = 6
MOD_ROWS = 8
LANES = 128
V7X_VMEM_BYTES = 64 * 1024 * 1024
VMEM_LIMIT = V7X_VMEM_BYTES - 8 * 1024 * 1024
LOG2E = 1.4426950408889634
MAX_SAFE_SHIFT = 60.0


def _cparams(*sem):
    return pltpu.CompilerParams(dimension_semantics=sem, vmem_limit_bytes=VMEM_LIMIT)


def _resident(shape, index_map):
    return pl.BlockSpec(shape, index_map, pipeline_mode=pl.Buffered(1))


def _largest_divisor(n, candidates):
    for c in candidates:
        if n % c == 0:
            return c
    raise ValueError(f"no tile in {candidates} divides {n}")


def _ln(x):
    mu = jnp.mean(x, axis=-1, keepdims=True)
    xc = x - mu
    var = jnp.mean(xc * xc, axis=-1, keepdims=True)
    return xc * lax.rsqrt(var + EPS)


def _silu(x):
    return x * jax.nn.sigmoid(x)


def _gelu_tanh(x):
    return 0.5 * x * (1.0 + jnp.tanh(math.sqrt(2.0 / math.pi) * (x + 0.044715 * (x * x * x))))


def _store_token_major(ref, x):
    t, d = x.shape
    g = d // LANES
    for c in range(g):
        ref[pl.ds(c, t, stride=g), :] = x[:, c * LANES:(c + 1) * LANES]


def _load_token_major(ref, t, g):
    return jnp.concatenate([ref[pl.ds(c, t, stride=g), :] for c in range(g)], axis=1)


def _ada_kernel(c_ref, w_ref, b_ref, o_ref):
    s = _silu(c_ref[...]).astype(BF16)
    o_ref[...] = jnp.dot(s, w_ref[...].astype(BF16), preferred_element_type=F32) + b_ref[...]


def _ada_mods(cond, ada_w, ada_b):
    depth, d, nd = ada_w.shape
    tn = _largest_divisor(nd, (1024, 512, 256, 128))
    out = pl.pallas_call(
        _ada_kernel,
        grid=(depth, nd // tn),
        in_specs=[
            pl.BlockSpec((MOD_ROWS, d), lambda l, j: (0, 0)),
            pl.BlockSpec((None, d, tn), lambda l, j: (l, 0, j)),
            pl.BlockSpec((None, 1, tn), lambda l, j: (l, 0, j)),
        ],
        out_specs=pl.BlockSpec((None, MOD_ROWS, tn), lambda l, j: (l, 0, j)),
        out_shape=jax.ShapeDtypeStruct((depth, MOD_ROWS, nd), F32),
        compiler_params=_cparams("parallel", "parallel"),
        name="ada_mods",
    )(cond, ada_w, ada_b.reshape(depth, 1, nd))
    return out.reshape(depth * MOD_ROWS * N_MODS, 1, d)


class _Dims:
    def __init__(self, x, ctx, ada_w, q_norm_g, sgu_w, ret_decay_fwd, ffn_w1, moe_w1):
        self.b, self.l, self.d = x.shape
        self.ctx = ctx.shape[1]
        self.depth = ada_w.shape[0]
        self.hd = q_norm_g.shape[1]
        self.mw = self.d // 2
        self.nq = self.mw // self.hd
        self.nkv = self.nq // GQA_GROUP
        self.kvw = self.nkv * self.hd
        self.sgu_groups, self.sgu_chunk = sgu_w.shape[1], sgu_w.shape[2]
        self.ret_heads = ret_decay_fwd.shape[1]
        self.ret_chunk = 128
        self.n_lat = self.b * self.l
        self.n_ctx = self.b * self.ctx
        self.n_tok = self.n_lat + self.n_ctx
        self.lk = self.ctx + self.l
        self.n_exp = moe_w1.shape[1]
        self.alpha = (2 * self.depth) ** 0.25
        mw = self.mw
        self.c_u, self.c_v = 0, mw
        self.c_rq, self.c_rk, self.c_rv, self.c_rg = 2 * mw, 3 * mw, 4 * mw, 5 * mw
        self.c_bg = 6 * mw
        self.c_aq = 6 * mw + 3 * self.d
        self.c_ak = self.c_aq + mw
        self.c_av = self.c_ak + self.kvw
        self.n_cols = self.c_av + self.kvw

    def mod_row(self, layer, j):
        def f(i, tm):
            bidx = jnp.minimum((i * tm) // self.l, self.b)
            return (layer * MOD_ROWS + bidx) * N_MODS + j
        return f


def _reorder_w_in(w, dm):
    mw, kvw = dm.mw, dm.kvw
    aq = w[:, :mw]
    ak = w[:, mw:mw + kvw]
    av = w[:, mw + kvw:mw + 2 * kvw]
    rest = w[:, mw + 2 * kvw:]
    return jnp.concatenate([rest, aq, ak, av], axis=1).astype(BF16)


def _proj_kernel(x_ref, sh_ref, sc_ref, w_ref, o_ref, h_ref):
    @pl.when(pl.program_id(1) == 0)
    def _():
        h = _ln(x_ref[...]) * (1.0 + sc_ref[...]) + sh_ref[...]
        h_ref[...] = h.astype(BF16)

    o_ref[...] = jnp.dot(h_ref[...], w_ref[...], preferred_element_type=F32).astype(o_ref.dtype)


def _in_projection(xs, mods, w, dm, layer):
    t, d = xs.shape
    nc = w.shape[1]
    tm = _largest_divisor(math.gcd(dm.l, dm.n_ctx), (1024, 512, 256, 128))
    tn = _largest_divisor(nc, (3 * dm.mw // 2, dm.mw // 2))
    row = dm.mod_row
    return pl.pallas_call(
        _proj_kernel,
        grid=(t // tm, nc // tn),
        in_specs=[
            pl.BlockSpec((tm, d), lambda i, j: (i, 0)),
            pl.BlockSpec((None, 1, d), lambda i, j: (row(layer, 0)(i, tm), 0, 0)),
            pl.BlockSpec((None, 1, d), lambda i, j: (row(layer, 1)(i, tm), 0, 0)),
            pl.BlockSpec((d, tn), lambda i, j: (0, j)),
        ],
        out_specs=pl.BlockSpec((tm, tn), lambda i, j: (i, j)),
        out_shape=jax.ShapeDtypeStruct((t, nc), BF16),
        scratch_shapes=[pltpu.VMEM((tm, d), BF16)],
        compiler_params=_cparams("parallel", "arbitrary"),
        name="in_projection",
    )(xs, mods, mods, w)


def _rope_tables(dm):
    n_freq = dm.hd // 4
    t = jnp.arange(dm.l, dtype=F32)
    row = jnp.floor(t / GRID_W)
    col = t - row * GRID_W
    inv_freq = ROPE_THETA ** (-jnp.arange(n_freq, dtype=F32) / n_freq)
    ar, ac = row[:, None] * inv_freq, col[:, None] * inv_freq
    z = jnp.zeros_like(ar)
    cos = jnp.concatenate([jnp.cos(ar), jnp.cos(ar), jnp.cos(ac), jnp.cos(ac)], axis=1)
    sin_lo = jnp.concatenate([z, jnp.sin(ar), z, jnp.sin(ac)], axis=1)
    sin_hi = jnp.concatenate([-jnp.sin(ar), z, -jnp.sin(ac), z], axis=1)
    pad = lambda a, v: jnp.concatenate([a, jnp.full((dm.ctx, dm.hd), v, F32)], axis=0)
    return pad(cos, 1.0), pad(sin_lo, 0.0), pad(sin_hi, 0.0)


def _prep_kernel(q_ref, k_ref, v_ref, qg_ref, kg_ref, cos_ref, slo_ref, shi_ref, qo_ref, kt_ref, vo_ref,
                 *, nq, nkv, hd, qscale):
    cos, slo, shi = cos_ref[...], slo_ref[...], shi_ref[...]
    quarter = hd // 4

    def norm_rope(xh, g):
        y = xh * lax.rsqrt(jnp.mean(xh * xh, axis=-1, keepdims=True) + EPS) * g
        return y * cos + pltpu.roll(y, quarter, 1) * slo + pltpu.roll(y, hd - quarter, 1) * shi

    qg = qg_ref[...] * qscale
    kg = kg_ref[...]
    for h in range(nq):
        sl = slice(h * hd, (h + 1) * hd)
        qo_ref[:, sl] = norm_rope(q_ref[:, sl].astype(F32), qg).astype(qo_ref.dtype)
    for g in range(nkv):
        sl = slice(g * hd, (g + 1) * hd)
        kt_ref[g] = norm_rope(k_ref[:, sl].astype(F32), kg).T.astype(kt_ref.dtype)
        vo_ref[g] = v_ref[:, sl]


def _attention_prep(proj, q_norm_g, k_norm_g, tables, dm):
    t = proj.shape[0]
    tp = _largest_divisor(math.gcd(dm.l, dm.ctx), (256, 128))
    mw, kvw, hd = dm.mw, dm.kvw, dm.hd
    n_lat_tiles = dm.n_lat // tp

    def batch_of(i):
        return jnp.where(i < n_lat_tiles, (i * tp) // dm.l, (i * tp - dm.n_lat) // dm.ctx)

    def key_block(i):
        lat = (dm.ctx + (i * tp) % dm.l) // tp
        ctx = ((i * tp - dm.n_lat) % dm.ctx) // tp
        return jnp.where(i < n_lat_tiles, lat, ctx)

    def table_block(i):
        lat = ((i * tp) % dm.l) // tp
        ctx = dm.l // tp + ((i * tp - dm.n_lat) % dm.ctx) // tp
        return jnp.where(i < n_lat_tiles, lat, ctx)

    tab_spec = pl.BlockSpec((tp, hd), lambda i: (table_block(i), 0))
    qscale = hd ** -0.5 * LOG2E
    return pl.pallas_call(
        functools.partial(_prep_kernel, nq=dm.nq, nkv=dm.nkv, hd=hd, qscale=qscale),
        grid=(t // tp,),
        in_specs=[
            pl.BlockSpec((tp, mw), lambda i: (i, dm.c_aq // mw)),
            pl.BlockSpec((tp, kvw), lambda i: (i, dm.c_ak // kvw)),
            pl.BlockSpec((tp, kvw), lambda i: (i, dm.c_av // kvw)),
            pl.BlockSpec((1, hd), lambda i: (0, 0)),
            pl.BlockSpec((1, hd), lambda i: (0, 0)),
            tab_spec, tab_spec, tab_spec,
        ],
        out_specs=[
            pl.BlockSpec((tp, mw), lambda i: (i, 0)),
            pl.BlockSpec((None, dm.nkv, hd, tp), lambda i: (batch_of(i), 0, 0, key_block(i))),
            pl.BlockSpec((None, dm.nkv, tp, hd), lambda i: (batch_of(i), 0, key_block(i), 0)),
        ],
        out_shape=[
            jax.ShapeDtypeStruct((t, mw), BF16),
            jax.ShapeDtypeStruct((dm.b, dm.nkv, hd, dm.lk), BF16),
            jax.ShapeDtypeStruct((dm.b, dm.nkv, dm.lk, hd), BF16),
        ],
        compiler_params=_cparams("parallel"),
        name="attention_prep",
    )(proj, proj, proj, q_norm_g.reshape(1, hd), k_norm_g.reshape(1, hd), *tables)


def _attn_kernel(q_ref, kt_ref, v_ref, o_ref, m_ref, l_ref, acc_ref, kmax_ref, *,
                 hd, n_lat_tiles, lat_keys, ctx_keys):
    tq = q_ref.shape[0]
    i = pl.program_id(2)
    q = q_ref[...]
    qs = jnp.concatenate([q[:, h * hd:(h + 1) * hd] for h in range(GQA_GROUP)], axis=0)

    @pl.when(i == 0)
    def _():
        k = kt_ref[...].astype(F32)
        k2 = jnp.max(jnp.sum(k * k, axis=0, keepdims=True), axis=1, keepdims=True)
        kmax_ref[...] = jnp.broadcast_to(jnp.sqrt(k2), kmax_ref.shape)

    qf = qs.astype(F32)
    shift = jnp.sqrt(jnp.sum(qf * qf, axis=-1, keepdims=True)) * kmax_ref[0:1, 0:1]
    bounded = jnp.max(shift) <= MAX_SAFE_SHIFT

    def finish(l):
        o = acc_ref[...] * (1.0 / l)
        o_ref[...] = jnp.concatenate([o[h * tq:(h + 1) * tq] for h in range(GQA_GROUP)],
                                     axis=1).astype(o_ref.dtype)

    def attend_bounded(tk, nk):
        cb = _largest_divisor(tk, (256, 128))
        m_ref[...] = jnp.broadcast_to(shift, m_ref.shape)
        l_ref[...] = jnp.zeros(l_ref.shape, F32)
        acc_ref[...] = jnp.zeros(acc_ref.shape, F32)

        def body(kt, carry):
            off = kt * tk
            mm = m_ref[...]
            mm = jnp.concatenate([mm] * (cb // 128), axis=1)
            lsum = l_ref[...]
            parts = []
            for c in range(tk // cb):
                col = pl.multiple_of(off + c * cb, cb)
                p = jnp.exp2(jnp.dot(qs, kt_ref[:, pl.ds(col, cb)], preferred_element_type=F32) - mm)
                for u in range(cb // 128):
                    lsum = lsum + p[:, u * 128:(u + 1) * 128]
                parts.append(p.astype(BF16))
            l_ref[...] = lsum
            acc_ref[...] += jnp.dot(jnp.concatenate(parts, axis=1), v_ref[pl.ds(pl.multiple_of(off, tk), tk), :],
                                    preferred_element_type=F32)
            return carry

        lax.fori_loop(0, nk, body, 0)
        finish(jnp.sum(l_ref[...], axis=-1, keepdims=True))

    def attend_online(tk, nk):
        m_ref[...] = jnp.full(m_ref.shape, -jnp.inf, F32)
        l_ref[...] = jnp.zeros(l_ref.shape, F32)
        acc_ref[...] = jnp.zeros(acc_ref.shape, F32)

        def body(kt, carry):
            off = pl.multiple_of(kt * tk, tk)
            s = jnp.dot(qs, kt_ref[:, pl.ds(off, tk)], preferred_element_type=F32)
            m_prev = m_ref[:, 0:1]
            m_new = jnp.maximum(m_prev, jnp.max(s, axis=-1, keepdims=True))
            p = jnp.exp2(s - m_new)
            a = jnp.exp2(m_prev - m_new)
            l_ref[:, 0:1] = a * l_ref[:, 0:1] + jnp.sum(p, axis=-1, keepdims=True)
            acc_ref[...] = a * acc_ref[...] + jnp.dot(p.astype(BF16), v_ref[pl.ds(off, tk), :],
                                                      preferred_element_type=F32)
            m_ref[:, 0:1] = m_new
            return carry

        lax.fori_loop(0, nk, body, 0)
        finish(l_ref[:, 0:1])

    def attend(keys):
        @pl.when(bounded)
        def _():
            attend_bounded(*keys)

        @pl.when(jnp.logical_not(bounded))
        def _():
            attend_online(*keys)

    is_latent = i < n_lat_tiles

    @pl.when(is_latent)
    def _():
        attend(lat_keys)

    if ctx_keys is not None:
        @pl.when(jnp.logical_not(is_latent))
        def _():
            attend(ctx_keys)


def _attention(qp, kt, v, dm, *, with_ctx):
    hd, mw = dm.hd, dm.mw
    tq = _largest_divisor(math.gcd(dm.l, dm.ctx), (256, 128))
    key_tiling = lambda n: (lambda tk: (tk, n // tk))(_largest_divisor(n, (768, 512, 256, 128)))
    gw = GQA_GROUP * hd
    nl, nc = dm.l // tq, dm.ctx // tq
    rows = dm.n_tok if with_ctx else dm.n_lat

    def q_map(b, g, i):
        return jnp.where(i < nl, b * nl + i, dm.n_lat // tq + b * nc + (i - nl)), g

    return pl.pallas_call(
        functools.partial(_attn_kernel, hd=hd, n_lat_tiles=nl, lat_keys=key_tiling(dm.lk),
                          ctx_keys=key_tiling(dm.ctx) if with_ctx else None),
        grid=(dm.b, dm.nkv, nl + (nc if with_ctx else 0)),
        in_specs=[
            pl.BlockSpec((tq, gw), q_map),
            pl.BlockSpec((None, None, hd, dm.lk), lambda b, g, i: (b, g, 0, 0)),
            pl.BlockSpec((None, None, dm.lk, hd), lambda b, g, i: (b, g, 0, 0)),
        ],
        out_specs=pl.BlockSpec((tq, gw), q_map),
        out_shape=jax.ShapeDtypeStruct((rows, mw), BF16),
        scratch_shapes=[
            pltpu.VMEM((GQA_GROUP * tq, 128), F32),
            pltpu.VMEM((GQA_GROUP * tq, 128), F32),
            pltpu.VMEM((GQA_GROUP * tq, hd), F32),
            pltpu.VMEM((8, 128), F32),
        ],
        compiler_params=_cparams("parallel", "parallel", "arbitrary"),
        name="attention",
    )(qp, kt, v)


def _sgu_kernel(u_ref, v_ref, g_ref, b_ref, w_ref, bs_ref, o_ref, *, ng, gw, chunk):
    ts = u_ref.shape[0]
    u = _gelu_tanh(u_ref[...].astype(F32))
    v = _ln(_gelu_tanh(v_ref[...].astype(F32))) * g_ref[...] + b_ref[...]
    vb = v.astype(BF16)
    for c in range(ts // chunk):
        rows = slice(c * chunk, (c + 1) * chunk)
        for g in range(ng):
            cols = slice(g * gw, (g + 1) * gw)
            z = jnp.dot(w_ref[g], vb[rows, cols], preferred_element_type=F32) + bs_ref[g]
            o_ref[rows, cols] = (u[rows, cols] * z).astype(o_ref.dtype)


def _sgu(proj, ln_g, ln_b, w_s, b_s, dm, rows):
    mw, chunk, ng = dm.mw, dm.sgu_chunk, dm.sgu_groups
    ts = _largest_divisor(math.gcd(dm.l, dm.n_ctx), (256, 128))
    return pl.pallas_call(
        functools.partial(_sgu_kernel, ng=ng, gw=mw // ng, chunk=chunk),
        grid=(rows // ts,),
        in_specs=[
            pl.BlockSpec((ts, mw), lambda i: (i, dm.c_u // mw)),
            pl.BlockSpec((ts, mw), lambda i: (i, dm.c_v // mw)),
            pl.BlockSpec((1, mw), lambda i: (0, 0)),
            pl.BlockSpec((1, mw), lambda i: (0, 0)),
            pl.BlockSpec((ng, chunk, chunk), lambda i: (0, 0, 0)),
            pl.BlockSpec((ng, chunk, 1), lambda i: (0, 0, 0)),
        ],
        out_specs=pl.BlockSpec((ts, mw), lambda i: (i, 0)),
        out_shape=jax.ShapeDtypeStruct((rows, mw), BF16),
        compiler_params=_cparams("parallel"),
        name="sgu",
    )(proj, proj, ln_g.reshape(1, mw), ln_b.reshape(1, mw), w_s.astype(BF16), b_s.reshape(ng, chunk, 1))


def _ret_tables(decay_logit, dm, backward):
    c, hd, nh = dm.ret_chunk, dm.hd, dm.ret_heads
    lg = jax.nn.log_sigmoid(decay_logit.astype(F32))
    pos = jnp.arange(c, dtype=F32)
    diff = pos[:, None] - pos[None, :]
    if backward:
        diff = -diff
        xi_e = c - pos
        zeta_e = pos
    else:
        xi_e = pos + 1.0
        zeta_e = c - 1.0 - pos
    dmat = jnp.where(diff >= 0, jnp.exp(jnp.maximum(diff, 0.0)[None] * lg[:, None, None]), 0.0)
    dmat = dmat * hd ** -0.5
    xi = jnp.exp(xi_e[None, :] * lg[:, None])[..., None]
    zeta = (jnp.exp(zeta_e[None, :] * lg[:, None]) * hd ** -0.5)[..., None]
    dec = jnp.broadcast_to(jnp.exp(c * lg)[:, None, None], (nh, 1, hd))
    return dmat.astype(F32), xi.astype(F32), zeta.astype(F32), dec.astype(F32)


def _ret_kernel(*refs, nh, hd, final):
    if final:
        q_ref, k_ref, v_ref, dm_ref, xi_ref, zeta_ref, dec_ref, of_ref, g_ref, o_ref, st_ref = refs
    else:
        q_ref, k_ref, v_ref, dm_ref, xi_ref, zeta_ref, dec_ref, o_ref, st_ref = refs

    @pl.when(pl.program_id(1) == 0)
    def _():
        st_ref[...] = jnp.zeros(st_ref.shape, F32)

    for h in range(nh):
        sl = slice(h * hd, (h + 1) * hd)
        qh, kh, vh = q_ref[:, sl], k_ref[:, sl], v_ref[:, sl]
        s = lax.dot_general(qh, kh, (((1,), (1,)), ((), ())), preferred_element_type=F32)
        inner = (s * dm_ref[h]).astype(BF16)
        st = st_ref[h]
        o = jnp.dot(inner, vh, preferred_element_type=F32)
        o = o + jnp.dot(qh, st.astype(BF16), preferred_element_type=F32) * xi_ref[h]
        kz = (kh.astype(F32) * zeta_ref[h]).astype(BF16)
        st_ref[h] = st * dec_ref[h] + lax.dot_general(kz, vh, (((0,), (0,)), ((), ())),
                                                      preferred_element_type=F32)
        if final:
            gn = _ln(of_ref[:, sl] + o)
            o_ref[:, sl] = (_silu(g_ref[:, sl].astype(F32)) * gn).astype(o_ref.dtype)
        else:
            o_ref[:, sl] = o


def _retention_pass(proj, tables, dm, *, backward, o_fwd=None):
    c, mw, nh, hd = dm.ret_chunk, dm.mw, dm.ret_heads, dm.hd
    n_cc, n_lc = dm.ctx // c, dm.l // c
    steps = n_cc + n_lc

    def row_block(b, t):
        if backward:
            ctx = dm.n_lat // c + b * n_cc + (n_cc - 1 - t)
            lat = b * n_lc + (n_lc - 1 - (t - n_cc))
        else:
            ctx = dm.n_lat // c + b * n_cc + t
            lat = b * n_lc + (t - n_cc)
        return jnp.where(t < n_cc, ctx, lat)

    col = lambda off: (lambda b, t: (row_block(b, t), off // mw))
    const3 = lambda b, t: (0, 0, 0)
    in_specs = [
        pl.BlockSpec((c, mw), col(dm.c_rq)),
        pl.BlockSpec((c, mw), col(dm.c_rk)),
        pl.BlockSpec((c, mw), col(dm.c_rv)),
        pl.BlockSpec((nh, c, c), const3),
        pl.BlockSpec((nh, c, 1), const3),
        pl.BlockSpec((nh, c, 1), const3),
        pl.BlockSpec((nh, 1, hd), const3),
    ]
    args = [proj, proj, proj, *tables]
    final = o_fwd is not None
    if final:
        in_specs += [pl.BlockSpec((c, mw), col(0)), pl.BlockSpec((c, mw), col(dm.c_rg))]
        args += [o_fwd, proj]
    return pl.pallas_call(
        functools.partial(_ret_kernel, nh=nh, hd=hd, final=final),
        grid=(dm.b, steps),
        in_specs=in_specs,
        out_specs=pl.BlockSpec((c, mw), col(0)),
        out_shape=jax.ShapeDtypeStruct((dm.n_tok, mw), BF16 if final else F32),
        scratch_shapes=[pltpu.VMEM((nh, hd, hd), F32)],
        compiler_params=_cparams("parallel", "arbitrary"),
        name="retention_bwd" if final else "retention_fwd",
    )(*args)


def _merge_kernel(att_ref, sgu_ref, ret_ref, ga_ref, gs_ref, gr_ref, wa_ref, ws_ref, wr_ref, wo_ref,
                  x_ref, g1_ref, lng_ref, lnb_ref, o_ref, *, alpha):
    def branch(a_ref, w_ref, gate_ref):
        return jax.nn.sigmoid(gate_ref[...].astype(F32)) * jnp.dot(a_ref[...], w_ref[...],
                                                                    preferred_element_type=F32)

    m = branch(att_ref, wa_ref, ga_ref) + branch(sgu_ref, ws_ref, gs_ref) + branch(ret_ref, wr_ref, gr_ref)
    mix = jnp.dot(m.astype(BF16), wo_ref[...], preferred_element_type=F32)
    y = alpha * x_ref[...] + g1_ref[...] * mix
    o_ref[...] = _ln(y) * lng_ref[...] + lnb_ref[...]


def _merge(att, sgu, ret, proj, w_att, w_sgu, w_ret, w_o, xs, mods, ln_g, ln_b, dm, layer, rows):
    d, mw = dm.d, dm.mw
    tm = _largest_divisor(math.gcd(dm.l, dm.n_ctx), (256, 128))
    row = dm.mod_row
    tok = lambda w: pl.BlockSpec((tm, w), lambda i: (i, 0))
    gate = lambda k: pl.BlockSpec((tm, d), lambda i: (i, dm.c_bg // d + k))
    vec = pl.BlockSpec((1, d), lambda i: (0, 0))
    return pl.pallas_call(
        functools.partial(_merge_kernel, alpha=dm.alpha),
        grid=(rows // tm,),
        in_specs=[
            tok(mw), tok(mw), tok(mw), gate(0), gate(1), gate(2),
            _resident((mw, d), lambda i: (0, 0)), _resident((mw, d), lambda i: (0, 0)),
            _resident((mw, d), lambda i: (0, 0)), _resident((d, d), lambda i: (0, 0)),
            tok(d),
            pl.BlockSpec((None, 1, d), lambda i: (row(layer, 2)(i, tm), 0, 0)),
            vec, vec,
        ],
        out_specs=tok(d),
        out_shape=jax.ShapeDtypeStruct((rows, d), F32),
        compiler_params=_cparams("parallel"),
        name="merge",
    )(att, sgu, ret, proj, proj, proj, w_att, w_sgu, w_ret, w_o, xs, mods,
      ln_g.reshape(1, d), ln_b.reshape(1, d))


def _ffn_kernel(x_ref, sh_ref, sc_ref, w1_ref, w3_ref, w2_ref, g2_ref, lng_ref, lnb_ref, o_ref,
                h_ref, acc_ref, *, alpha):
    j = pl.program_id(1)

    @pl.when(j == 0)
    def _():
        h = _ln(x_ref[...]) * (1.0 + sc_ref[...]) + sh_ref[...]
        h_ref[...] = h.astype(BF16)
        acc_ref[...] = jnp.zeros(acc_ref.shape, F32)

    h = h_ref[...]
    a = jnp.dot(h, w1_ref[...], preferred_element_type=F32)
    b = jnp.dot(h, w3_ref[...], preferred_element_type=F32)
    acc_ref[...] += jnp.dot((_silu(a) * b).astype(BF16), w2_ref[...], preferred_element_type=F32)

    @pl.when(j == pl.num_programs(1) - 1)
    def _():
        y = alpha * x_ref[...] + g2_ref[...] * acc_ref[...]
        o_ref[...] = _ln(y) * lng_ref[...] + lnb_ref[...]


def _ffn(xs, mods, w1, w3, w2, ln_g, ln_b, dm, layer):
    t, d = xs.shape
    f = w1.shape[1]
    tm = _largest_divisor(math.gcd(dm.l, dm.n_ctx), (512, 256, 128))
    tf = _largest_divisor(f, (512, 256, 128))
    row = dm.mod_row
    mod = lambda k: pl.BlockSpec((None, 1, d), lambda i, j: (row(layer, k)(i, tm), 0, 0))
    vec = pl.BlockSpec((1, d), lambda i, j: (0, 0))
    return pl.pallas_call(
        functools.partial(_ffn_kernel, alpha=dm.alpha),
        grid=(t // tm, f // tf),
        in_specs=[
            pl.BlockSpec((tm, d), lambda i, j: (i, 0)),
            mod(3), mod(4),
            pl.BlockSpec((d, tf), lambda i, j: (0, j)),
            pl.BlockSpec((d, tf), lambda i, j: (0, j)),
            pl.BlockSpec((tf, d), lambda i, j: (j, 0)),
            mod(5), vec, vec,
        ],
        out_specs=pl.BlockSpec((tm, d), lambda i, j: (i, 0)),
        out_shape=jax.ShapeDtypeStruct((t, d), F32),
        scratch_shapes=[pltpu.VMEM((tm, d), BF16), pltpu.VMEM((tm, d), F32)],
        compiler_params=_cparams("parallel", "arbitrary"),
        name="ffn_dense",
    )(xs, mods, mods, w1, w3, w2, mods, ln_g.reshape(1, d), ln_b.reshape(1, d))


def _router_kernel(x_ref, sh_ref, sc_ref, whi_ref, wlo_ref, h_ref, gates_ref, code_ref):
    h = _ln(x_ref[...]) * (1.0 + sc_ref[...]) + sh_ref[...]
    _store_token_major(h_ref, h)
    h_hi = h.astype(BF16)
    h_lo = (h - h_hi.astype(F32)).astype(BF16)
    w_hi = whi_ref[...]
    logits = (jnp.dot(h_hi, w_hi, preferred_element_type=F32)
              + jnp.dot(h_lo, w_hi, preferred_element_type=F32)
              + jnp.dot(h_hi, wlo_ref[...], preferred_element_type=F32))
    n_exp = float(logits.shape[-1])
    ids = lax.broadcasted_iota(jnp.int32, logits.shape, 1).astype(F32)
    m1 = jnp.max(logits, axis=-1, keepdims=True)
    i1 = jnp.min(jnp.where(logits == m1, ids, n_exp), axis=-1, keepdims=True)
    rest = jnp.where(ids == i1, -jnp.inf, logits)
    m2 = jnp.max(rest, axis=-1, keepdims=True)
    i2 = jnp.min(jnp.where(rest == m2, ids, n_exp), axis=-1, keepdims=True)
    e = jnp.exp(m2 - m1)
    w1 = 1.0 / (1.0 + e)
    w2 = e * w1
    gates_ref[...] = jnp.where(ids == i1, w1, jnp.where(ids == i2, w2, 0.0))
    code_ref[...] = jnp.where(ids == i1, 1, jnp.where(ids == i2, 2, 0)).astype(jnp.int32)


def _router(xs, mods, router_w, dm, layer, rows):
    d, n_exp = router_w.shape
    tm = _largest_divisor(dm.l, (512, 256, 128))
    row = dm.mod_row
    mod = lambda k: pl.BlockSpec((None, 1, d), lambda i: (row(layer, k)(i, tm), 0, 0))
    w_hi = router_w.astype(BF16)
    w_lo = (router_w - w_hi.astype(F32)).astype(BF16)
    return pl.pallas_call(
        _router_kernel,
        grid=(rows // tm,),
        in_specs=[
            pl.BlockSpec((tm, d), lambda i: (i, 0)), mod(3), mod(4),
            pl.BlockSpec((d, n_exp), lambda i: (0, 0)),
            pl.BlockSpec((d, n_exp), lambda i: (0, 0)),
        ],
        out_specs=[
            pl.BlockSpec((tm * (d // LANES), LANES), lambda i: (i, 0)),
            pl.BlockSpec((tm, n_exp), lambda i: (i, 0)),
            pl.BlockSpec((tm, n_exp), lambda i: (i, 0)),
        ],
        out_shape=[
            jax.ShapeDtypeStruct((rows * (d // LANES), LANES), F32),
            jax.ShapeDtypeStruct((rows, n_exp), F32),
            jax.ShapeDtypeStruct((rows, n_exp), jnp.int32),
        ],
        compiler_params=_cparams("parallel"),
        name="router",
    )(xs, mods, mods, w_hi, w_lo)


def _token_rows(ref, t, g):
    return ref.at[pl.ds(pl.multiple_of(t * g, g), g), :]


def _row_gather(index_of, lo, hi, hbm_ref, buf_ref, sem, g, *, wait):
    def body(r, carry):
        cp = pltpu.make_async_copy(_token_rows(hbm_ref, index_of(r), g), _token_rows(buf_ref, r, g), sem)
        if wait:
            cp.wait()
        else:
            cp.start()
        return carry

    lax.fori_loop(lo, hi, body, 0)


def _row_scatter(index_of, lo, hi, buf_ref, hbm_ref, sem, g, *, wait):
    def body(r, carry):
        t = index_of(r)

        @pl.when(t >= 0)
        def _():
            cp = pltpu.make_async_copy(_token_rows(buf_ref, r, g), _token_rows(hbm_ref, t, g), sem)
            if wait:
                cp.wait()
            else:
                cp.start()

        return carry

    lax.fori_loop(lo, hi, body, 0)


def _expert_kernel(te_ref, tv_ref, tok_cur, tok_nxt, dst_prv, h_hbm, w1_ref, w3_ref, w2_ref, y_hbm,
                   in_buf, out_buf, in_sem, out_sem, h_ref, acc_ref, *, nj):
    i, j = pl.program_id(0), pl.program_id(1)
    nt = pl.num_programs(0)
    tm = h_ref.shape[0]
    g = h_ref.shape[1] // LANES
    slot = i % 2
    chunk = -(-tm // nj)
    lo, hi = jnp.minimum(j * chunk, tm), jnp.minimum((j + 1) * chunk, tm)
    valid = tv_ref[i] > 0
    prev_valid = jnp.logical_and(i > 0, tv_ref[jnp.maximum(i - 1, 0)] > 0)
    next_valid = jnp.logical_and(i + 1 < nt, tv_ref[jnp.minimum(i + 1, nt - 1)] > 0)
    gather = functools.partial(_row_gather, hbm_ref=h_hbm, g=g)
    scatter = functools.partial(_row_scatter, lambda r: dst_prv[0, 0, r], buf_ref=out_buf.at[1 - slot],
                                hbm_ref=y_hbm, sem=out_sem.at[1 - slot], g=g)

    @pl.when(jnp.logical_and(valid, j == 0))
    def _():
        @pl.when(i == 0)
        def _():
            gather(lambda r: tok_cur[0, 0, r], 0, tm, buf_ref=in_buf.at[0], sem=in_sem.at[0], wait=False)

        gather(lambda r: tok_cur[0, 0, r], 0, tm, buf_ref=in_buf.at[slot], sem=in_sem.at[slot], wait=True)
        h_ref[...] = _load_token_major(in_buf.at[slot], tm, g).astype(BF16)
        acc_ref[...] = jnp.zeros(acc_ref.shape, F32)

    @pl.when(next_valid)
    def _():
        gather(lambda r: tok_nxt[0, 0, r], lo, hi, buf_ref=in_buf.at[1 - slot], sem=in_sem.at[1 - slot], wait=False)

    @pl.when(prev_valid)
    def _():
        scatter(lo, hi, wait=False)

    @pl.when(valid)
    def _():
        h = h_ref[...]
        a = jnp.dot(h, w1_ref[...], preferred_element_type=F32)
        b = jnp.dot(h, w3_ref[...], preferred_element_type=F32)
        acc_ref[...] += jnp.dot((_silu(a) * b).astype(BF16), w2_ref[...], preferred_element_type=F32)

        @pl.when(j == nj - 1)
        def _():
            _store_token_major(out_buf.at[slot], acc_ref[...])

    @pl.when(jnp.logical_and(prev_valid, j == nj - 1))
    def _():
        scatter(0, tm, wait=True)


def _expert_mlp(h_tm, slot_token, slot_dest, tile_expert, tile_valid, w1, w3, w2, tm, d, n_out):
    n_slots = slot_token.shape[0]
    f = w1.shape[2]
    g = d // LANES
    tf = _largest_divisor(f, (512, 256, 128))
    nj, nt = f // tf, n_slots // tm
    jj = lambda i, j, tv: jnp.where(tv[i] > 0, j, nj - 1)
    tok = slot_token.reshape(nt, 1, tm)
    dst = slot_dest.reshape(nt, 1, tm)
    smem = lambda f_i: pl.BlockSpec((1, 1, tm), lambda i, j, te, tv: (f_i(i), 0, 0), memory_space=pltpu.SMEM)
    grid_spec = pltpu.PrefetchScalarGridSpec(
        num_scalar_prefetch=2,
        grid=(nt, nj),
        in_specs=[
            smem(lambda i: i),
            smem(lambda i: jnp.minimum(i + 1, nt - 1)),
            smem(lambda i: jnp.maximum(i - 1, 0)),
            pl.BlockSpec(memory_space=pl.ANY),
            pl.BlockSpec((None, d, tf), lambda i, j, te, tv: (te[i], 0, jj(i, j, tv))),
            pl.BlockSpec((None, d, tf), lambda i, j, te, tv: (te[i], 0, jj(i, j, tv))),
            pl.BlockSpec((None, tf, d), lambda i, j, te, tv: (te[i], jj(i, j, tv), 0)),
        ],
        out_specs=pl.BlockSpec(memory_space=pl.ANY),
        scratch_shapes=[
            pltpu.VMEM((2, tm * g, LANES), F32),
            pltpu.VMEM((2, tm * g, LANES), F32),
            pltpu.SemaphoreType.DMA((2,)),
            pltpu.SemaphoreType.DMA((2,)),
            pltpu.VMEM((tm, d), BF16),
            pltpu.VMEM((tm, d), F32),
        ],
    )
    return pl.pallas_call(
        functools.partial(_expert_kernel, nj=nj),
        grid_spec=grid_spec,
        out_shape=jax.ShapeDtypeStruct((n_out * g, LANES), F32),
        compiler_params=_cparams("arbitrary", "arbitrary"),
        name="expert_mlp",
    )(tile_expert, tile_valid, tok, tok, dst, h_tm, w1, w3, w2)


def _moe_out_kernel(y1_ref, y2_ref, gates_ref, code_ref, x_ref, g2_ref, lng_ref, lnb_ref, o_ref, *, alpha):
    tm, d = x_ref.shape
    g = d // LANES
    gates, code = gates_ref[...], code_ref[...]
    f = jnp.zeros((tm, d), F32)
    for k, y_ref in enumerate((y1_ref, y2_ref)):
        w = jnp.sum(jnp.where(code == k + 1, gates, 0.0), axis=-1, keepdims=True)
        f = f + w * _load_token_major(y_ref, tm, g)
    y = alpha * x_ref[...] + g2_ref[...] * f
    o_ref[...] = _ln(y) * lng_ref[...] + lnb_ref[...]


def _moe_out(y_tm, gates, code, xs, mods, ln_g, ln_b, dm, layer, rows):
    d, n_exp = dm.d, gates.shape[1]
    g = d // LANES
    tm = _largest_divisor(dm.l, (256, 128))
    nt = rows // tm
    row = dm.mod_row
    vec = pl.BlockSpec((1, d), lambda i: (0, 0))
    return pl.pallas_call(
        functools.partial(_moe_out_kernel, alpha=dm.alpha),
        grid=(nt,),
        in_specs=[
            pl.BlockSpec((tm * g, LANES), lambda i: (i, 0)),
            pl.BlockSpec((tm * g, LANES), lambda i: (nt + i, 0)),
            pl.BlockSpec((tm, n_exp), lambda i: (i, 0)),
            pl.BlockSpec((tm, n_exp), lambda i: (i, 0)),
            pl.BlockSpec((tm, d), lambda i: (i, 0)),
            pl.BlockSpec((None, 1, d), lambda i: (row(layer, 5)(i, tm), 0, 0)),
            vec, vec,
        ],
        out_specs=pl.BlockSpec((tm, d), lambda i: (i, 0)),
        out_shape=jax.ShapeDtypeStruct((rows, d), F32),
        compiler_params=_cparams("parallel"),
        name="moe_out",
    )(y_tm, y_tm, gates, code, xs, mods, ln_g.reshape(1, d), ln_b.reshape(1, d))


def _moe(xs, mods, router_w, w1, w3, w2, ln_g, ln_b, dm, layer, rows):
    n_exp = router_w.shape[1]
    tm = 512 if rows * TOP_K >= 512 * n_exp * 4 else 128
    h, gates, code = _router(xs, mods, router_w, dm, layer, rows)

    onehot = jnp.concatenate([code == 1, code == 2], axis=0).astype(jnp.int32)
    csum = jnp.cumsum(onehot, axis=0)
    counts = csum[-1]
    rank = jnp.sum(csum * onehot, axis=1) - 1
    padded = ((counts + tm - 1) // tm) * tm
    ends = jnp.cumsum(padded)
    starts = ends - padded
    slot = (jnp.sum(onehot * starts[None, :], axis=1) + rank).astype(jnp.int32)
    n_slots = TOP_K * rows + (n_exp + 1) * tm
    assignment = jnp.arange(TOP_K * rows, dtype=jnp.int32)
    slot_dest = jnp.full((n_slots,), -1, jnp.int32).at[slot].set(assignment)
    slot_token = jnp.maximum(slot_dest, 0) % rows
    tile_start = jnp.arange(n_slots // tm, dtype=jnp.int32) * tm
    tile_valid = (tile_start < ends[-1]).astype(jnp.int32)
    last_used = jnp.minimum(tile_start, ends[-1] - 1)
    tile_expert = jnp.minimum(jnp.sum((ends[None, :] <= last_used[:, None]).astype(jnp.int32), axis=1),
                              n_exp - 1).astype(jnp.int32)

    y = _expert_mlp(h, slot_token, slot_dest, tile_expert, tile_valid, w1, w3, w2, tm, dm.d, TOP_K * rows)
    return _moe_out(y, gates, code, xs, mods, ln_g, ln_b, dm, layer, rows)


def kernel(x, c, ctx, c_ctx, ada_w, ada_b, w_in, q_norm_g, k_norm_g, sgu_ln_g, sgu_ln_b, sgu_w, sgu_b,
           ret_decay_fwd, ret_decay_bwd, w_br_att, w_br_sgu, w_br_ret, w_out, ln1_g, ln1_b, ln2_g, ln2_b,
           ffn_w1, ffn_w3, ffn_w2, router_w, moe_w1, moe_w3, moe_w2):
    dm = _Dims(x, ctx, ada_w, q_norm_g, sgu_w, ret_decay_fwd, ffn_w1, moe_w1)
    assert dm.b + 1 <= MOD_ROWS and dm.ret_heads == dm.nq
    d = dm.d

    cond = jnp.concatenate([c, c_ctx[None], jnp.zeros((MOD_ROWS - dm.b - 1, d), F32)], axis=0)
    mods = _ada_mods(cond, ada_w, ada_b)
    rope = _rope_tables(dm)

    xs = jnp.concatenate([x.reshape(dm.n_lat, d), ctx.reshape(dm.n_ctx, d)], axis=0)
    for i in range(dm.depth):
        last = i == dm.depth - 1
        rows = dm.n_lat if last else dm.n_tok

        proj = _in_projection(xs, mods, _reorder_w_in(w_in[i], dm), dm, i)

        qp, kt, v_all = _attention_prep(proj, q_norm_g[i], k_norm_g[i], rope, dm)
        att = _attention(qp, kt, v_all, dm, with_ctx=not last)

        sgu = _sgu(proj, sgu_ln_g[i], sgu_ln_b[i], sgu_w[i], sgu_b[i], dm, rows)

        o_fwd = _retention_pass(proj, _ret_tables(ret_decay_fwd[i], dm, False), dm, backward=False)
        ret = _retention_pass(proj, _ret_tables(ret_decay_bwd[i], dm, True), dm, backward=True, o_fwd=o_fwd)

        xs1 = _merge(att, sgu, ret, proj, w_br_att[i].astype(BF16), w_br_sgu[i].astype(BF16),
                     w_br_ret[i].astype(BF16), w_out[i].astype(BF16), xs, mods, ln1_g[i], ln1_b[i], dm, i, rows)

        j = i // 2
        if i % 2 == 0:
            assert rows == xs1.shape[0]
            xs = _ffn(xs1, mods, ffn_w1[j].astype(BF16), ffn_w3[j].astype(BF16), ffn_w2[j].astype(BF16),
                      ln2_g[i], ln2_b[i], dm, i)
        else:
            xs = _moe(xs1, mods, router_w[j], moe_w1[j].astype(BF16), moe_w3[j].astype(BF16),
                      moe_w2[j].astype(BF16), ln2_g[i], ln2_b[i], dm, i, rows)
    return xs[:dm.n_lat].reshape(dm.b, dm.l, d)
```

```python
import functools
import math

import jax
import jax.numpy as jnp
from jax import lax
from jax.experimental import pallas as pl
from jax.experimental.pallas import tpu as pltpu

F32 = jnp.float32
BF16 = jnp.bfloat16
EPS = 1e-6
GRID_W = 64
ROPE_THETA = 10000.0
GQA_GROUP = 4
TOP_K = 2
N_MODS = 6
MOD_ROWS = 8
LANES = 128
DMA_QUEUES = 2
V7X_VMEM_BYTES = 64 * 1024 * 1024
VMEM_LIMIT = V7X_VMEM_BYTES - 8 * 1024 * 1024
LOG2E = 1.4426950408889634
MAX_SAFE_SHIFT = 60.0


def _cparams(*sem):
    return pltpu.CompilerParams(dimension_semantics=sem, vmem_limit_bytes=VMEM_LIMIT)


def _resident(shape, index_map):
    return pl.BlockSpec(shape, index_map, pipeline_mode=pl.Buffered(1))


def _largest_divisor(n, candidates):
    for c in candidates:
        if n % c == 0:
            return c
    raise ValueError(f"no tile in {candidates} divides {n}")


def _ln(x):
    mu = jnp.mean(x, axis=-1, keepdims=True)
    xc = x - mu
    var = jnp.mean(xc * xc, axis=-1, keepdims=True)
    return xc * lax.rsqrt(var + EPS)


def _silu(x):
    return x * jax.nn.sigmoid(x)


def _gelu_tanh(x):
    return 0.5 * x * (1.0 + jnp.tanh(math.sqrt(2.0 / math.pi) * (x + 0.044715 * (x * x * x))))


def _store_token_major(ref, x):
    t, d = x.shape
    g = d // LANES
    for c in range(g):
        ref[pl.ds(c, t, stride=g), :] = x[:, c * LANES:(c + 1) * LANES]


def _load_token_major(ref, t, g):
    return jnp.concatenate([ref[pl.ds(c, t, stride=g), :] for c in range(g)], axis=1)


def _ada_kernel(c_ref, w_ref, b_ref, o_ref):
    s = _silu(c_ref[...]).astype(BF16)
    o_ref[...] = jnp.dot(s, w_ref[...].astype(BF16), preferred_element_type=F32) + b_ref[...]


def _ada_mods(cond, ada_w, ada_b):
    depth, d, nd = ada_w.shape
    tn = _largest_divisor(nd, (1024, 512, 256, 128))
    out = pl.pallas_call(
        _ada_kernel,
        grid=(depth, nd // tn),
        in_specs=[
            pl.BlockSpec((MOD_ROWS, d), lambda l, j: (0, 0)),
            pl.BlockSpec((None, d, tn), lambda l, j: (l, 0, j)),
            pl.BlockSpec((None, 1, tn), lambda l, j: (l, 0, j)),
        ],
        out_specs=pl.BlockSpec((None, MOD_ROWS, tn), lambda l, j: (l, 0, j)),
        out_shape=jax.ShapeDtypeStruct((depth, MOD_ROWS, nd), F32),
        compiler_params=_cparams("parallel", "parallel"),
        name="ada_mods",
    )(cond, ada_w, ada_b.reshape(depth, 1, nd))
    return out.reshape(depth * MOD_ROWS * N_MODS, 1, d)


class _Dims:
    def __init__(self, x, ctx, ada_w, q_norm_g, sgu_w, ret_decay_fwd, ffn_w1, moe_w1):
        self.b, self.l, self.d = x.shape
        self.ctx = ctx.shape[1]
        self.depth = ada_w.shape[0]
        self.hd = q_norm_g.shape[1]
        self.mw = self.d // 2
        self.nq = self.mw // self.hd
        self.nkv = self.nq // GQA_GROUP
        self.kvw = self.nkv * self.hd
        self.sgu_groups, self.sgu_chunk = sgu_w.shape[1], sgu_w.shape[2]
        self.ret_heads = ret_decay_fwd.shape[1]
        self.ret_chunk = 128
        self.n_lat = self.b * self.l
        self.n_ctx = self.b * self.ctx
        self.n_tok = self.n_lat + self.n_ctx
        self.lk = self.ctx + self.l
        self.n_exp = moe_w1.shape[1]
        self.alpha = (2 * self.depth) ** 0.25
        mw = self.mw
        self.c_u, self.c_v = 0, mw
        self.c_rq, self.c_rk, self.c_rv, self.c_rg = 2 * mw, 3 * mw, 4 * mw, 5 * mw
        self.c_bg = 6 * mw
        self.c_aq = 6 * mw + 3 * self.d
        self.c_ak = self.c_aq + mw
        self.c_av = self.c_ak + self.kvw
        self.n_cols = self.c_av + self.kvw

    def mod_row(self, layer, j):
        def f(i, tm):
            bidx = jnp.minimum((i * tm) // self.l, self.b)
            return (layer * MOD_ROWS + bidx) * N_MODS + j
        return f


def _reorder_w_in(w, dm):
    mw, kvw = dm.mw, dm.kvw
    aq = w[:, :mw]
    ak = w[:, mw:mw + kvw]
    av = w[:, mw + kvw:mw + 2 * kvw]
    rest = w[:, mw + 2 * kvw:]
    return jnp.concatenate([rest, aq, ak, av], axis=1).astype(BF16)


def _proj_kernel(x_ref, sh_ref, sc_ref, w_ref, o_ref, h_ref):
    @pl.when(pl.program_id(1) == 0)
    def _():
        h = _ln(x_ref[...]) * (1.0 + sc_ref[...]) + sh_ref[...]
        h_ref[...] = h.astype(BF16)

    o_ref[...] = jnp.dot(h_ref[...], w_ref[...], preferred_element_type=F32).astype(o_ref.dtype)


def _in_projection(xs, mods, w, dm, layer):
    t, d = xs.shape
    nc = w.shape[1]
    tm = _largest_divisor(math.gcd(dm.l, dm.n_ctx), (1024, 512, 256, 128))
    tn = _largest_divisor(nc, (3 * dm.mw // 2, dm.mw // 2))
    row = dm.mod_row
    return pl.pallas_call(
        _proj_kernel,
        grid=(t // tm, nc // tn),
        in_specs=[
            pl.BlockSpec((tm, d), lambda i, j: (i, 0)),
            pl.BlockSpec((None, 1, d), lambda i, j: (row(layer, 0)(i, tm), 0, 0)),
            pl.BlockSpec((None, 1, d), lambda i, j: (row(layer, 1)(i, tm), 0, 0)),
            pl.BlockSpec((d, tn), lambda i, j: (0, j)),
        ],
        out_specs=pl.BlockSpec((tm, tn), lambda i, j: (i, j)),
        out_shape=jax.ShapeDtypeStruct((t, nc), BF16),
        scratch_shapes=[pltpu.VMEM((tm, d), BF16)],
        compiler_params=_cparams("parallel", "arbitrary"),
        name="in_projection",
    )(xs, mods, mods, w)


def _rope_tables(dm):
    n_freq = dm.hd // 4
    t = jnp.arange(dm.l, dtype=F32)
    row = jnp.floor(t / GRID_W)
    col = t - row * GRID_W
    inv_freq = ROPE_THETA ** (-jnp.arange(n_freq, dtype=F32) / n_freq)
    ar, ac = row[:, None] * inv_freq, col[:, None] * inv_freq
    z = jnp.zeros_like(ar)
    cos = jnp.concatenate([jnp.cos(ar), jnp.cos(ar), jnp.cos(ac), jnp.cos(ac)], axis=1)
    sin_lo = jnp.concatenate([z, jnp.sin(ar), z, jnp.sin(ac)], axis=1)
    sin_hi = jnp.concatenate([-jnp.sin(ar), z, -jnp.sin(ac), z], axis=1)
    pad = lambda a, v: jnp.concatenate([a, jnp.full((dm.ctx, dm.hd), v, F32)], axis=0)
    return pad(cos, 1.0), pad(sin_lo, 0.0), pad(sin_hi, 0.0)


def _prep_kernel(q_ref, k_ref, v_ref, qg_ref, kg_ref, cos_ref, slo_ref, shi_ref, qo_ref, kt_ref, vo_ref,
                 *, nq, nkv, hd, qscale):
    cos, slo, shi = cos_ref[...], slo_ref[...], shi_ref[...]
    quarter = hd // 4

    def norm_rope(xh, g):
        y = xh * lax.rsqrt(jnp.mean(xh * xh, axis=-1, keepdims=True) + EPS) * g
        return y * cos + pltpu.roll(y, quarter, 1) * slo + pltpu.roll(y, hd - quarter, 1) * shi

    qg = qg_ref[...] * qscale
    kg = kg_ref[...]
    for h in range(nq):
        sl = slice(h * hd, (h + 1) * hd)
        qo_ref[:, sl] = norm_rope(q_ref[:, sl].astype(F32), qg).astype(qo_ref.dtype)
    for g in range(nkv):
        sl = slice(g * hd, (g + 1) * hd)
        kt_ref[g] = norm_rope(k_ref[:, sl].astype(F32), kg).T.astype(kt_ref.dtype)
        vo_ref[g] = v_ref[:, sl]


def _attention_prep(proj, q_norm_g, k_norm_g, tables, dm):
    t = proj.shape[0]
    tp = _largest_divisor(math.gcd(dm.l, dm.ctx), (256, 128))
    mw, kvw, hd = dm.mw, dm.kvw, dm.hd
    n_lat_tiles = dm.n_lat // tp

    def batch_of(i):
        return jnp.where(i < n_lat_tiles, (i * tp) // dm.l, (i * tp - dm.n_lat) // dm.ctx)

    def key_block(i):
        lat = (dm.ctx + (i * tp) % dm.l) // tp
        ctx = ((i * tp - dm.n_lat) % dm.ctx) // tp
        return jnp.where(i < n_lat_tiles, lat, ctx)

    def table_block(i):
        lat = ((i * tp) % dm.l) // tp
        ctx = dm.l // tp + ((i * tp - dm.n_lat) % dm.ctx) // tp
        return jnp.where(i < n_lat_tiles, lat, ctx)

    tab_spec = pl.BlockSpec((tp, hd), lambda i: (table_block(i), 0))
    qscale = hd ** -0.5 * LOG2E
    return pl.pallas_call(
        functools.partial(_prep_kernel, nq=dm.nq, nkv=dm.nkv, hd=hd, qscale=qscale),
        grid=(t // tp,),
        in_specs=[
            pl.BlockSpec((tp, mw), lambda i: (i, dm.c_aq // mw)),
            pl.BlockSpec((tp, kvw), lambda i: (i, dm.c_ak // kvw)),
            pl.BlockSpec((tp, kvw), lambda i: (i, dm.c_av // kvw)),
            pl.BlockSpec((1, hd), lambda i: (0, 0)),
            pl.BlockSpec((1, hd), lambda i: (0, 0)),
            tab_spec, tab_spec, tab_spec,
        ],
        out_specs=[
            pl.BlockSpec((tp, mw), lambda i: (i, 0)),
            pl.BlockSpec((None, dm.nkv, hd, tp), lambda i: (batch_of(i), 0, 0, key_block(i))),
            pl.BlockSpec((None, dm.nkv, tp, hd), lambda i: (batch_of(i), 0, key_block(i), 0)),
        ],
        out_shape=[
            jax.ShapeDtypeStruct((t, mw), BF16),
            jax.ShapeDtypeStruct((dm.b, dm.nkv, hd, dm.lk), BF16),
            jax.ShapeDtypeStruct((dm.b, dm.nkv, dm.lk, hd), BF16),
        ],
        compiler_params=_cparams("parallel"),
        name="attention_prep",
    )(proj, proj, proj, q_norm_g.reshape(1, hd), k_norm_g.reshape(1, hd), *tables)


def _attn_kernel(q_ref, kt_ref, v_ref, o_ref, m_ref, l_ref, acc_ref, kmax_ref, *,
                 hd, n_lat_tiles, lat_keys, ctx_keys):
    tq = q_ref.shape[0]
    i = pl.program_id(2)
    q = q_ref[...]
    qs = jnp.concatenate([q[:, h * hd:(h + 1) * hd] for h in range(GQA_GROUP)], axis=0)

    @pl.when(i == 0)
    def _():
        k = kt_ref[...].astype(F32)
        k2 = jnp.max(jnp.sum(k * k, axis=0, keepdims=True), axis=1, keepdims=True)
        kmax_ref[...] = jnp.broadcast_to(jnp.sqrt(k2), kmax_ref.shape)

    qf = qs.astype(F32)
    shift = jnp.sqrt(jnp.sum(qf * qf, axis=-1, keepdims=True)) * kmax_ref[0:1, 0:1]
    bounded = jnp.max(shift) <= MAX_SAFE_SHIFT

    def finish(l):
        o = acc_ref[...] * (1.0 / l)
        o_ref[...] = jnp.concatenate([o[h * tq:(h + 1) * tq] for h in range(GQA_GROUP)],
                                     axis=1).astype(o_ref.dtype)

    def attend_bounded(tk, nk):
        cb = _largest_divisor(tk, (256, 128))
        m_ref[...] = jnp.broadcast_to(shift, m_ref.shape)
        l_ref[...] = jnp.zeros(l_ref.shape, F32)
        acc_ref[...] = jnp.zeros(acc_ref.shape, F32)

        def body(kt, carry):
            off = kt * tk
            mm = m_ref[...]
            mm = jnp.concatenate([mm] * (cb // 128), axis=1)
            lsum = l_ref[...]
            parts = []
            for c in range(tk // cb):
                col = pl.multiple_of(off + c * cb, cb)
                p = jnp.exp2(jnp.dot(qs, kt_ref[:, pl.ds(col, cb)], preferred_element_type=F32) - mm)
                for u in range(cb // 128):
                    lsum = lsum + p[:, u * 128:(u + 1) * 128]
                parts.append(p.astype(BF16))
            l_ref[...] = lsum
            acc_ref[...] += jnp.dot(jnp.concatenate(parts, axis=1), v_ref[pl.ds(pl.multiple_of(off, tk), tk), :],
                                    preferred_element_type=F32)
            return carry

        lax.fori_loop(0, nk, body, 0)
        finish(jnp.sum(l_ref[...], axis=-1, keepdims=True))

    def attend_online(tk, nk):
        m_ref[...] = jnp.full(m_ref.shape, -jnp.inf, F32)
        l_ref[...] = jnp.zeros(l_ref.shape, F32)
        acc_ref[...] = jnp.zeros(acc_ref.shape, F32)

        def body(kt, carry):
            off = pl.multiple_of(kt * tk, tk)
            s = jnp.dot(qs, kt_ref[:, pl.ds(off, tk)], preferred_element_type=F32)
            m_prev = m_ref[:, 0:1]
            m_new = jnp.maximum(m_prev, jnp.max(s, axis=-1, keepdims=True))
            p = jnp.exp2(s - m_new)
            a = jnp.exp2(m_prev - m_new)
            l_ref[:, 0:1] = a * l_ref[:, 0:1] + jnp.sum(p, axis=-1, keepdims=True)
            acc_ref[...] = a * acc_ref[...] + jnp.dot(p.astype(BF16), v_ref[pl.ds(off, tk), :],
                                                      preferred_element_type=F32)
            m_ref[:, 0:1] = m_new
            return carry

        lax.fori_loop(0, nk, body, 0)
        finish(l_ref[:, 0:1])

    def attend(keys):
        @pl.when(bounded)
        def _():
            attend_bounded(*keys)

        @pl.when(jnp.logical_not(bounded))
        def _():
            attend_online(*keys)

    is_latent = i < n_lat_tiles

    @pl.when(is_latent)
    def _():
        attend(lat_keys)

    if ctx_keys is not None:
        @pl.when(jnp.logical_not(is_latent))
        def _():
            attend(ctx_keys)


def _attention(qp, kt, v, dm, *, with_ctx):
    hd, mw = dm.hd, dm.mw
    tq = _largest_divisor(math.gcd(dm.l, dm.ctx), (256, 128))
    key_tiling = lambda n: (lambda tk: (tk, n // tk))(_largest_divisor(n, (768, 512, 256, 128)))
    gw = GQA_GROUP * hd
    nl, nc = dm.l // tq, dm.ctx // tq
    rows = dm.n_tok if with_ctx else dm.n_lat

    def q_map(b, g, i):
        return jnp.where(i < nl, b * nl + i, dm.n_lat // tq + b * nc + (i - nl)), g

    return pl.pallas_call(
        functools.partial(_attn_kernel, hd=hd, n_lat_tiles=nl, lat_keys=key_tiling(dm.lk),
                          ctx_keys=key_tiling(dm.ctx) if with_ctx else None),
        grid=(dm.b, dm.nkv, nl + (nc if with_ctx else 0)),
        in_specs=[
            pl.BlockSpec((tq, gw), q_map),
            pl.BlockSpec((None, None, hd, dm.lk), lambda b, g, i: (b, g, 0, 0)),
            pl.BlockSpec((None, None, dm.lk, hd), lambda b, g, i: (b, g, 0, 0)),
        ],
        out_specs=pl.BlockSpec((tq, gw), q_map),
        out_shape=jax.ShapeDtypeStruct((rows, mw), BF16),
        scratch_shapes=[
            pltpu.VMEM((GQA_GROUP * tq, 128), F32),
            pltpu.VMEM((GQA_GROUP * tq, 128), F32),
            pltpu.VMEM((GQA_GROUP * tq, hd), F32),
            pltpu.VMEM((8, 128), F32),
        ],
        compiler_params=_cparams("parallel", "parallel", "arbitrary"),
        name="attention",
    )(qp, kt, v)


def _sgu_kernel(u_ref, v_ref, g_ref, b_ref, w_ref, bs_ref, o_ref, *, ng, gw, chunk):
    ts = u_ref.shape[0]
    u = _gelu_tanh(u_ref[...].astype(F32))
    v = _ln(_gelu_tanh(v_ref[...].astype(F32))) * g_ref[...] + b_ref[...]
    vb = v.astype(BF16)
    for c in range(ts // chunk):
        rows = slice(c * chunk, (c + 1) * chunk)
        for g in range(ng):
            cols = slice(g * gw, (g + 1) * gw)
            z = jnp.dot(w_ref[g], vb[rows, cols], preferred_element_type=F32) + bs_ref[g]
            o_ref[rows, cols] = (u[rows, cols] * z).astype(o_ref.dtype)


def _sgu(proj, ln_g, ln_b, w_s, b_s, dm, rows):
    mw, chunk, ng = dm.mw, dm.sgu_chunk, dm.sgu_groups
    ts = _largest_divisor(math.gcd(dm.l, dm.n_ctx), (256, 128))
    return pl.pallas_call(
        functools.partial(_sgu_kernel, ng=ng, gw=mw // ng, chunk=chunk),
        grid=(rows // ts,),
        in_specs=[
            pl.BlockSpec((ts, mw), lambda i: (i, dm.c_u // mw)),
            pl.BlockSpec((ts, mw), lambda i: (i, dm.c_v // mw)),
            pl.BlockSpec((1, mw), lambda i: (0, 0)),
            pl.BlockSpec((1, mw), lambda i: (0, 0)),
            pl.BlockSpec((ng, chunk, chunk), lambda i: (0, 0, 0)),
            pl.BlockSpec((ng, chunk, 1), lambda i: (0, 0, 0)),
        ],
        out_specs=pl.BlockSpec((ts, mw), lambda i: (i, 0)),
        out_shape=jax.ShapeDtypeStruct((rows, mw), BF16),
        compiler_params=_cparams("parallel"),
        name="sgu",
    )(proj, proj, ln_g.reshape(1, mw), ln_b.reshape(1, mw), w_s.astype(BF16), b_s.reshape(ng, chunk, 1))


def _ret_tables(decay_logit, dm, backward):
    c, hd, nh = dm.ret_chunk, dm.hd, dm.ret_heads
    lg = jax.nn.log_sigmoid(decay_logit.astype(F32))
    pos = jnp.arange(c, dtype=F32)
    diff = pos[:, None] - pos[None, :]
    if backward:
        diff = -diff
        xi_e = c - pos
        zeta_e = pos
    else:
        xi_e = pos + 1.0
        zeta_e = c - 1.0 - pos
    dmat = jnp.where(diff >= 0, jnp.exp(jnp.maximum(diff, 0.0)[None] * lg[:, None, None]), 0.0)
    dmat = dmat * hd ** -0.5
    xi = jnp.exp(xi_e[None, :] * lg[:, None])[..., None]
    zeta = (jnp.exp(zeta_e[None, :] * lg[:, None]) * hd ** -0.5)[..., None]
    dec = jnp.broadcast_to(jnp.exp(c * lg)[:, None, None], (nh, 1, hd))
    return dmat.astype(F32), xi.astype(F32), zeta.astype(F32), dec.astype(F32)


def _ret_kernel(*refs, nh, hd, final):
    if final:
        q_ref, k_ref, v_ref, dm_ref, xi_ref, zeta_ref, dec_ref, of_ref, g_ref, o_ref, st_ref = refs
    else:
        q_ref, k_ref, v_ref, dm_ref, xi_ref, zeta_ref, dec_ref, o_ref, st_ref = refs

    @pl.when(pl.program_id(1) == 0)
    def _():
        st_ref[...] = jnp.zeros(st_ref.shape, F32)

    for h in range(nh):
        sl = slice(h * hd, (h + 1) * hd)
        qh, kh, vh = q_ref[:, sl], k_ref[:, sl], v_ref[:, sl]
        s = lax.dot_general(qh, kh, (((1,), (1,)), ((), ())), preferred_element_type=F32)
        inner = (s * dm_ref[h]).astype(BF16)
        st = st_ref[h]
        o = jnp.dot(inner, vh, preferred_element_type=F32)
        o = o + jnp.dot(qh, st.astype(BF16), preferred_element_type=F32) * xi_ref[h]
        kz = (kh.astype(F32) * zeta_ref[h]).astype(BF16)
        st_ref[h] = st * dec_ref[h] + lax.dot_general(kz, vh, (((0,), (0,)), ((), ())),
                                                      preferred_element_type=F32)
        if final:
            gn = _ln(of_ref[:, sl] + o)
            o_ref[:, sl] = (_silu(g_ref[:, sl].astype(F32)) * gn).astype(o_ref.dtype)
        else:
            o_ref[:, sl] = o


def _retention_pass(proj, tables, dm, *, backward, o_fwd=None):
    c, mw, nh, hd = dm.ret_chunk, dm.mw, dm.ret_heads, dm.hd
    n_cc, n_lc = dm.ctx // c, dm.l // c
    steps = n_cc + n_lc

    def row_block(b, t):
        if backward:
            ctx = dm.n_lat // c + b * n_cc + (n_cc - 1 - t)
            lat = b * n_lc + (n_lc - 1 - (t - n_cc))
        else:
            ctx = dm.n_lat // c + b * n_cc + t
            lat = b * n_lc + (t - n_cc)
        return jnp.where(t < n_cc, ctx, lat)

    col = lambda off: (lambda b, t: (row_block(b, t), off // mw))
    const3 = lambda b, t: (0, 0, 0)
    in_specs = [
        pl.BlockSpec((c, mw), col(dm.c_rq)),
        pl.BlockSpec((c, mw), col(dm.c_rk)),
        pl.BlockSpec((c, mw), col(dm.c_rv)),
        pl.BlockSpec((nh, c, c), const3),
        pl.BlockSpec((nh, c, 1), const3),
        pl.BlockSpec((nh, c, 1), const3),
        pl.BlockSpec((nh, 1, hd), const3),
    ]
    args = [proj, proj, proj, *tables]
    final = o_fwd is not None
    if final:
        in_specs += [pl.BlockSpec((c, mw), col(0)), pl.BlockSpec((c, mw), col(dm.c_rg))]
        args += [o_fwd, proj]
    return pl.pallas_call(
        functools.partial(_ret_kernel, nh=nh, hd=hd, final=final),
        grid=(dm.b, steps),
        in_specs=in_specs,
        out_specs=pl.BlockSpec((c, mw), col(0)),
        out_shape=jax.ShapeDtypeStruct((dm.n_tok, mw), BF16 if final else F32),
        scratch_shapes=[pltpu.VMEM((nh, hd, hd), F32)],
        compiler_params=_cparams("parallel", "arbitrary"),
        name="retention_bwd" if final else "retention_fwd",
    )(*args)


def _merge_kernel(att_ref, sgu_ref, ret_ref, ga_ref, gs_ref, gr_ref, wa_ref, ws_ref, wr_ref, wo_ref,
                  x_ref, g1_ref, lng_ref, lnb_ref, o_ref, *, alpha):
    def branch(a_ref, w_ref, gate_ref):
        return jax.nn.sigmoid(gate_ref[...].astype(F32)) * jnp.dot(a_ref[...], w_ref[...],
                                                                    preferred_element_type=F32)

    m = branch(att_ref, wa_ref, ga_ref) + branch(sgu_ref, ws_ref, gs_ref) + branch(ret_ref, wr_ref, gr_ref)
    mix = jnp.dot(m.astype(BF16), wo_ref[...], preferred_element_type=F32)
    y = alpha * x_ref[...] + g1_ref[...] * mix
    o_ref[...] = _ln(y) * lng_ref[...] + lnb_ref[...]


def _merge(att, sgu, ret, proj, w_att, w_sgu, w_ret, w_o, xs, mods, ln_g, ln_b, dm, layer, rows):
    d, mw = dm.d, dm.mw
    tm = _largest_divisor(math.gcd(dm.l, dm.n_ctx), (256, 128))
    row = dm.mod_row
    tok = lambda w: pl.BlockSpec((tm, w), lambda i: (i, 0))
    gate = lambda k: pl.BlockSpec((tm, d), lambda i: (i, dm.c_bg // d + k))
    vec = pl.BlockSpec((1, d), lambda i: (0, 0))
    return pl.pallas_call(
        functools.partial(_merge_kernel, alpha=dm.alpha),
        grid=(rows // tm,),
        in_specs=[
            tok(mw), tok(mw), tok(mw), gate(0), gate(1), gate(2),
            _resident((mw, d), lambda i: (0, 0)), _resident((mw, d), lambda i: (0, 0)),
            _resident((mw, d), lambda i: (0, 0)), _resident((d, d), lambda i: (0, 0)),
            tok(d),
            pl.BlockSpec((None, 1, d), lambda i: (row(layer, 2)(i, tm), 0, 0)),
            vec, vec,
        ],
        out_specs=tok(d),
        out_shape=jax.ShapeDtypeStruct((rows, d), F32),
        compiler_params=_cparams("parallel"),
        name="merge",
    )(att, sgu, ret, proj, proj, proj, w_att, w_sgu, w_ret, w_o, xs, mods,
      ln_g.reshape(1, d), ln_b.reshape(1, d))


def _ffn_kernel(x_ref, sh_ref, sc_ref, w1_ref, w3_ref, w2_ref, g2_ref, lng_ref, lnb_ref, o_ref,
                h_ref, acc_ref, *, alpha):
    j = pl.program_id(1)

    @pl.when(j == 0)
    def _():
        h = _ln(x_ref[...]) * (1.0 + sc_ref[...]) + sh_ref[...]
        h_ref[...] = h.astype(BF16)
        acc_ref[...] = jnp.zeros(acc_ref.shape, F32)

    h = h_ref[...]
    a = jnp.dot(h, w1_ref[...], preferred_element_type=F32)
    b = jnp.dot(h, w3_ref[...], preferred_element_type=F32)
    acc_ref[...] += jnp.dot((_silu(a) * b).astype(BF16), w2_ref[...], preferred_element_type=F32)

    @pl.when(j == pl.num_programs(1) - 1)
    def _():
        y = alpha * x_ref[...] + g2_ref[...] * acc_ref[...]
        o_ref[...] = _ln(y) * lng_ref[...] + lnb_ref[...]


def _ffn(xs, mods, w1, w3, w2, ln_g, ln_b, dm, layer):
    t, d = xs.shape
    f = w1.shape[1]
    tm = _largest_divisor(math.gcd(dm.l, dm.n_ctx), (512, 256, 128))
    tf = _largest_divisor(f, (512, 256, 128))
    row = dm.mod_row
    mod = lambda k: pl.BlockSpec((None, 1, d), lambda i, j: (row(layer, k)(i, tm), 0, 0))
    vec = pl.BlockSpec((1, d), lambda i, j: (0, 0))
    return pl.pallas_call(
        functools.partial(_ffn_kernel, alpha=dm.alpha),
        grid=(t // tm, f // tf),
        in_specs=[
            pl.BlockSpec((tm, d), lambda i, j: (i, 0)),
            mod(3), mod(4),
            pl.BlockSpec((d, tf), lambda i, j: (0, j)),
            pl.BlockSpec((d, tf), lambda i, j: (0, j)),
            pl.BlockSpec((tf, d), lambda i, j: (j, 0)),
            mod(5), vec, vec,
        ],
        out_specs=pl.BlockSpec((tm, d), lambda i, j: (i, 0)),
        out_shape=jax.ShapeDtypeStruct((t, d), F32),
        scratch_shapes=[pltpu.VMEM((tm, d), BF16), pltpu.VMEM((tm, d), F32)],
        compiler_params=_cparams("parallel", "arbitrary"),
        name="ffn_dense",
    )(xs, mods, mods, w1, w3, w2, mods, ln_g.reshape(1, d), ln_b.reshape(1, d))


def _router_kernel(x_ref, sh_ref, sc_ref, whi_ref, wlo_ref, h_ref, gates_ref, code_ref):
    h = _ln(x_ref[...]) * (1.0 + sc_ref[...]) + sh_ref[...]
    _store_token_major(h_ref, h)
    h_hi = h.astype(BF16)
    h_lo = (h - h_hi.astype(F32)).astype(BF16)
    w_hi = whi_ref[...]
    logits = (jnp.dot(h_hi, w_hi, preferred_element_type=F32)
              + jnp.dot(h_lo, w_hi, preferred_element_type=F32)
              + jnp.dot(h_hi, wlo_ref[...], preferred_element_type=F32))
    n_exp = float(logits.shape[-1])
    ids = lax.broadcasted_iota(jnp.int32, logits.shape, 1).astype(F32)
    m1 = jnp.max(logits, axis=-1, keepdims=True)
    i1 = jnp.min(jnp.where(logits == m1, ids, n_exp), axis=-1, keepdims=True)
    rest = jnp.where(ids == i1, -jnp.inf, logits)
    m2 = jnp.max(rest, axis=-1, keepdims=True)
    i2 = jnp.min(jnp.where(rest == m2, ids, n_exp), axis=-1, keepdims=True)
    e = jnp.exp(m2 - m1)
    w1 = 1.0 / (1.0 + e)
    w2 = e * w1
    gates_ref[...] = jnp.where(ids == i1, w1, jnp.where(ids == i2, w2, 0.0))
    code_ref[...] = jnp.where(ids == i1, 1, jnp.where(ids == i2, 2, 0)).astype(jnp.int32)


def _router(xs, mods, router_w, dm, layer, rows):
    d, n_exp = router_w.shape
    tm = _largest_divisor(dm.l, (512, 256, 128))
    row = dm.mod_row
    mod = lambda k: pl.BlockSpec((None, 1, d), lambda i: (row(layer, k)(i, tm), 0, 0))
    w_hi = router_w.astype(BF16)
    w_lo = (router_w - w_hi.astype(F32)).astype(BF16)
    return pl.pallas_call(
        _router_kernel,
        grid=(rows // tm,),
        in_specs=[
            pl.BlockSpec((tm, d), lambda i: (i, 0)), mod(3), mod(4),
            pl.BlockSpec((d, n_exp), lambda i: (0, 0)),
            pl.BlockSpec((d, n_exp), lambda i: (0, 0)),
        ],
        out_specs=[
            pl.BlockSpec((tm * (d // LANES), LANES), lambda i: (i, 0)),
            pl.BlockSpec((tm, n_exp), lambda i: (i, 0)),
            pl.BlockSpec((tm, n_exp), lambda i: (i, 0)),
        ],
        out_shape=[
            jax.ShapeDtypeStruct((rows * (d // LANES), LANES), F32),
            jax.ShapeDtypeStruct((rows, n_exp), F32),
            jax.ShapeDtypeStruct((rows, n_exp), jnp.int32),
        ],
        compiler_params=_cparams("parallel"),
        name="router",
    )(xs, mods, mods, w_hi, w_lo)


def _token_rows(ref, t, g):
    return ref.at[pl.ds(pl.multiple_of(t * g, g), g), :]


def _row_gather(index_of, lo, hi, hbm_ref, buf_ref, sem, g, *, wait):
    def body(q, carry):
        for u in range(DMA_QUEUES):
            r = lo + q * DMA_QUEUES + u
            cp = pltpu.make_async_copy(_token_rows(hbm_ref, index_of(r), g), _token_rows(buf_ref, r, g), sem)
            if wait:
                cp.wait()
            else:
                cp.start(priority=u)
        return carry

    lax.fori_loop(0, (hi - lo) // DMA_QUEUES, body, 0)


def _row_scatter(index_of, lo, hi, buf_ref, hbm_ref, sem, g, *, wait):
    def body(q, carry):
        for u in range(DMA_QUEUES):
            r = lo + q * DMA_QUEUES + u
            t = index_of(r)

            @pl.when(t >= 0)
            def _():
                cp = pltpu.make_async_copy(_token_rows(buf_ref, r, g), _token_rows(hbm_ref, t, g), sem)
                if wait:
                    cp.wait()
                else:
                    cp.start(priority=u)

        return carry

    lax.fori_loop(0, (hi - lo) // DMA_QUEUES, body, 0)


def _expert_kernel(te_ref, tv_ref, tok_cur, tok_nxt, dst_prv, h_hbm, w1_ref, w3_ref, w2_ref, y_hbm,
                   in_buf, out_buf, in_sem, out_sem, h_ref, acc_ref, *, nj):
    i, j = pl.program_id(0), pl.program_id(1)
    nt = pl.num_programs(0)
    tm = h_ref.shape[0]
    g = h_ref.shape[1] // LANES
    slot = i % 2
    chunk = -(-tm // (nj * DMA_QUEUES)) * DMA_QUEUES
    lo, hi = jnp.minimum(j * chunk, tm), jnp.minimum((j + 1) * chunk, tm)
    valid = tv_ref[i] > 0
    prev_valid = jnp.logical_and(i > 0, tv_ref[jnp.maximum(i - 1, 0)] > 0)
    next_valid = jnp.logical_and(i + 1 < nt, tv_ref[jnp.minimum(i + 1, nt - 1)] > 0)
    gather = functools.partial(_row_gather, hbm_ref=h_hbm, g=g)
    scatter = functools.partial(_row_scatter, lambda r: dst_prv[0, 0, r], buf_ref=out_buf.at[1 - slot],
                                hbm_ref=y_hbm, sem=out_sem.at[1 - slot], g=g)

    @pl.when(jnp.logical_and(valid, j == 0))
    def _():
        @pl.when(i == 0)
        def _():
            gather(lambda r: tok_cur[0, 0, r], 0, tm, buf_ref=in_buf.at[0], sem=in_sem.at[0], wait=False)

        gather(lambda r: tok_cur[0, 0, r], 0, tm, buf_ref=in_buf.at[slot], sem=in_sem.at[slot], wait=True)
        h_ref[...] = _load_token_major(in_buf.at[slot], tm, g).astype(BF16)
        acc_ref[...] = jnp.zeros(acc_ref.shape, F32)

    @pl.when(next_valid)
    def _():
        gather(lambda r: tok_nxt[0, 0, r], lo, hi, buf_ref=in_buf.at[1 - slot], sem=in_sem.at[1 - slot], wait=False)

    @pl.when(prev_valid)
    def _():
        scatter(lo, hi, wait=False)

    @pl.when(valid)
    def _():
        h = h_ref[...]
        a = jnp.dot(h, w1_ref[...], preferred_element_type=F32)
        b = jnp.dot(h, w3_ref[...], preferred_element_type=F32)
        acc_ref[...] += jnp.dot((_silu(a) * b).astype(BF16), w2_ref[...], preferred_element_type=F32)

        @pl.when(j == nj - 1)
        def _():
            _store_token_major(out_buf.at[slot], acc_ref[...])

    @pl.when(jnp.logical_and(prev_valid, j == nj - 1))
    def _():
        scatter(0, tm, wait=True)


def _expert_mlp(h_tm, slot_token, slot_dest, tile_expert, tile_valid, w1, w3, w2, tm, d, n_out):
    n_slots = slot_token.shape[0]
    f = w1.shape[2]
    g = d // LANES
    tf = _largest_divisor(f, (1024, 512, 256, 128))
    nj, nt = f // tf, n_slots // tm
    jj = lambda i, j, tv: jnp.where(tv[i] > 0, j, nj - 1)
    tok = slot_token.reshape(nt, 1, tm)
    dst = slot_dest.reshape(nt, 1, tm)
    smem = lambda f_i: pl.BlockSpec((1, 1, tm), lambda i, j, te, tv: (f_i(i), 0, 0), memory_space=pltpu.SMEM)
    grid_spec = pltpu.PrefetchScalarGridSpec(
        num_scalar_prefetch=2,
        grid=(nt, nj),
        in_specs=[
            smem(lambda i: i),
            smem(lambda i: jnp.minimum(i + 1, nt - 1)),
            smem(lambda i: jnp.maximum(i - 1, 0)),
            pl.BlockSpec(memory_space=pl.ANY),
            pl.BlockSpec((None, d, tf), lambda i, j, te, tv: (te[i], 0, jj(i, j, tv))),
            pl.BlockSpec((None, d, tf), lambda i, j, te, tv: (te[i], 0, jj(i, j, tv))),
            pl.BlockSpec((None, tf, d), lambda i, j, te, tv: (te[i], jj(i, j, tv), 0)),
        ],
        out_specs=pl.BlockSpec(memory_space=pl.ANY),
        scratch_shapes=[
            pltpu.VMEM((2, tm * g, LANES), F32),
            pltpu.VMEM((2, tm * g, LANES), F32),
            pltpu.SemaphoreType.DMA((2,)),
            pltpu.SemaphoreType.DMA((2,)),
            pltpu.VMEM((tm, d), BF16),
            pltpu.VMEM((tm, d), F32),
        ],
    )
    return pl.pallas_call(
        functools.partial(_expert_kernel, nj=nj),
        grid_spec=grid_spec,
        out_shape=jax.ShapeDtypeStruct((n_out * g, LANES), F32),
        compiler_params=_cparams("arbitrary", "arbitrary"),
        name="expert_mlp",
    )(tile_expert, tile_valid, tok, tok, dst, h_tm, w1, w3, w2)


def _moe_out_kernel(y1_ref, y2_ref, gates_ref, code_ref, x_ref, g2_ref, lng_ref, lnb_ref, o_ref, *, alpha):
    tm, d = x_ref.shape
    g = d // LANES
    gates, code = gates_ref[...], code_ref[...]
    f = jnp.zeros((tm, d), F32)
    for k, y_ref in enumerate((y1_ref, y2_ref)):
        w = jnp.sum(jnp.where(code == k + 1, gates, 0.0), axis=-1, keepdims=True)
        f = f + w * _load_token_major(y_ref, tm, g)
    y = alpha * x_ref[...] + g2_ref[...] * f
    o_ref[...] = _ln(y) * lng_ref[...] + lnb_ref[...]


def _moe_out(y_tm, gates, code, xs, mods, ln_g, ln_b, dm, layer, rows):
    d, n_exp = dm.d, gates.shape[1]
    g = d // LANES
    tm = _largest_divisor(dm.l, (256, 128))
    nt = rows // tm
    row = dm.mod_row
    vec = pl.BlockSpec((1, d), lambda i: (0, 0))
    return pl.pallas_call(
        functools.partial(_moe_out_kernel, alpha=dm.alpha),
        grid=(nt,),
        in_specs=[
            pl.BlockSpec((tm * g, LANES), lambda i: (i, 0)),
            pl.BlockSpec((tm * g, LANES), lambda i: (nt + i, 0)),
            pl.BlockSpec((tm, n_exp), lambda i: (i, 0)),
            pl.BlockSpec((tm, n_exp), lambda i: (i, 0)),
            pl.BlockSpec((tm, d), lambda i: (i, 0)),
            pl.BlockSpec((None, 1, d), lambda i: (row(layer, 5)(i, tm), 0, 0)),
            vec, vec,
        ],
        out_specs=pl.BlockSpec((tm, d), lambda i: (i, 0)),
        out_shape=jax.ShapeDtypeStruct((rows, d), F32),
        compiler_params=_cparams("parallel"),
        name="moe_out",
    )(y_tm, y_tm, gates, code, xs, mods, ln_g.reshape(1, d), ln_b.reshape(1, d))


def _moe(xs, mods, router_w, w1, w3, w2, ln_g, ln_b, dm, layer, rows):
    n_exp = router_w.shape[1]
    tm = 512 if rows * TOP_K >= 512 * n_exp * 4 else 128
    h, gates, code = _router(xs, mods, router_w, dm, layer, rows)

    onehot = jnp.concatenate([code == 1, code == 2], axis=0).astype(jnp.int32)
    csum = jnp.cumsum(onehot, axis=0)
    counts = csum[-1]
    rank = jnp.sum(csum * onehot, axis=1) - 1
    padded = ((counts + tm - 1) // tm) * tm
    ends = jnp.cumsum(padded)
    starts = ends - padded
    slot = (jnp.sum(onehot * starts[None, :], axis=1) + rank).astype(jnp.int32)
    n_slots = TOP_K * rows + (n_exp + 1) * tm
    assignment = jnp.arange(TOP_K * rows, dtype=jnp.int32)
    slot_dest = jnp.full((n_slots,), -1, jnp.int32).at[slot].set(assignment)
    slot_token = jnp.maximum(slot_dest, 0) % rows
    tile_start = jnp.arange(n_slots // tm, dtype=jnp.int32) * tm
    tile_valid = (tile_start < ends[-1]).astype(jnp.int32)
    last_used = jnp.minimum(tile_start, ends[-1] - 1)
    tile_expert = jnp.minimum(jnp.sum((ends[None, :] <= last_used[:, None]).astype(jnp.int32), axis=1),
                              n_exp - 1).astype(jnp.int32)

    y = _expert_mlp(h, slot_token, slot_dest, tile_expert, tile_valid, w1, w3, w2, tm, dm.d, TOP_K * rows)
    return _moe_out(y, gates, code, xs, mods, ln_g, ln_b, dm, layer, rows)


def kernel(x, c, ctx, c_ctx, ada_w, ada_b, w_in, q_norm_g, k_norm_g, sgu_ln_g, sgu_ln_b, sgu_w, sgu_b,
           ret_decay_fwd, ret_decay_bwd, w_br_att, w_br_sgu, w_br_ret, w_out, ln1_g, ln1_b, ln2_g, ln2_b,
           ffn_w1, ffn_w3, ffn_w2, router_w, moe_w1, moe_w3, moe_w2):
    dm = _Dims(x, ctx, ada_w, q_norm_g, sgu_w, ret_decay_fwd, ffn_w1, moe_w1)
    assert dm.b + 1 <= MOD_ROWS and dm.ret_heads == dm.nq
    d = dm.d

    cond = jnp.concatenate([c, c_ctx[None], jnp.zeros((MOD_ROWS - dm.b - 1, d), F32)], axis=0)
    mods = _ada_mods(cond, ada_w, ada_b)
    rope = _rope_tables(dm)

    xs = jnp.concatenate([x.reshape(dm.n_lat, d), ctx.reshape(dm.n_ctx, d)], axis=0)
    for i in range(dm.depth):
        last = i == dm.depth - 1
        rows = dm.n_lat if last else dm.n_tok

        proj = _in_projection(xs, mods, _reorder_w_in(w_in[i], dm), dm, i)

        qp, kt, v_all = _attention_prep(proj, q_norm_g[i], k_norm_g[i], rope, dm)
        att = _attention(qp, kt, v_all, dm, with_ctx=not last)

        sgu = _sgu(proj, sgu_ln_g[i], sgu_ln_b[i], sgu_w[i], sgu_b[i], dm, rows)

        o_fwd = _retention_pass(proj, _ret_tables(ret_decay_fwd[i], dm, False), dm, backward=False)
        ret = _retention_pass(proj, _ret_tables(ret_decay_bwd[i], dm, True), dm, backward=True, o_fwd=o_fwd)

        xs1 = _merge(att, sgu, ret, proj, w_br_att[i].astype(BF16), w_br_sgu[i].astype(BF16),
                     w_br_ret[i].astype(BF16), w_out[i].astype(BF16), xs, mods, ln1_g[i], ln1_b[i], dm, i, rows)

        j = i // 2
        if i % 2 == 0:
            assert rows == xs1.shape[0]
            xs = _ffn(xs1, mods, ffn_w1[j].astype(BF16), ffn_w3[j].astype(BF16), ffn_w2[j].astype(BF16),
                      ln2_g[i], ln2_b[i], dm, i)
        else:
            xs = _moe(xs1, mods, router_w[j], moe_w1[j].astype(BF16), moe_w3[j].astype(BF16),
                      moe_w2[j].astype(BF16), ln2_g[i], ln2_b[i], dm, i, rows)
    return xs[:dm.n_lat].reshape(dm.b, dm.l, d)
```

```python
import functools
import math

import jax
import jax.numpy as jnp
from jax import lax
from jax.experimental import pallas as pl
from jax.experimental.pallas import tpu as pltpu

F32 = jnp.float32
BF16 = jnp.bfloat16
EPS = 1e-6
GRID_W = 64
ROPE_THETA = 10000.0
GQA_GROUP = 4
TOP_K = 2
N_MODS = 6
MOD_ROWS = 8
LANES = 128
ROW_DMA_UNROLL = 2
V7X_VMEM_BYTES = 64 * 1024 * 1024
VMEM_LIMIT = V7X_VMEM_BYTES - 8 * 1024 * 1024
LOG2E = 1.4426950408889634
MAX_SAFE_SHIFT = 60.0


def _cparams(*sem):
    return pltpu.CompilerParams(dimension_semantics=sem, vmem_limit_bytes=VMEM_LIMIT)


def _resident(shape, index_map):
    return pl.BlockSpec(shape, index_map, pipeline_mode=pl.Buffered(1))


def _largest_divisor(n, candidates):
    for c in candidates:
        if n % c == 0:
            return c
    raise ValueError(f"no tile in {candidates} divides {n}")


def _ln(x):
    mu = jnp.mean(x, axis=-1, keepdims=True)
    xc = x - mu
    var = jnp.mean(xc * xc, axis=-1, keepdims=True)
    return xc * lax.rsqrt(var + EPS)


def _silu(x):
    return x * jax.nn.sigmoid(x)


def _gelu_tanh(x):
    return 0.5 * x * (1.0 + jnp.tanh(math.sqrt(2.0 / math.pi) * (x + 0.044715 * (x * x * x))))


def _store_token_major(ref, x):
    t, d = x.shape
    g = d // LANES
    for c in range(g):
        ref[pl.ds(c, t, stride=g), :] = x[:, c * LANES:(c + 1) * LANES]


def _load_token_major(ref, t, g):
    return jnp.concatenate([ref[pl.ds(c, t, stride=g), :] for c in range(g)], axis=1)


def _ada_kernel(c_ref, w_ref, b_ref, o_ref):
    s = _silu(c_ref[...]).astype(BF16)
    o_ref[...] = jnp.dot(s, w_ref[...].astype(BF16), preferred_element_type=F32) + b_ref[...]


def _ada_mods(cond, ada_w, ada_b):
    depth, d, nd = ada_w.shape
    tn = _largest_divisor(nd, (1024, 512, 256, 128))
    out = pl.pallas_call(
        _ada_kernel,
        grid=(depth, nd // tn),
        in_specs=[
            pl.BlockSpec((MOD_ROWS, d), lambda l, j: (0, 0)),
            pl.BlockSpec((None, d, tn), lambda l, j: (l, 0, j)),
            pl.BlockSpec((None, 1, tn), lambda l, j: (l, 0, j)),
        ],
        out_specs=pl.BlockSpec((None, MOD_ROWS, tn), lambda l, j: (l, 0, j)),
        out_shape=jax.ShapeDtypeStruct((depth, MOD_ROWS, nd), F32),
        compiler_params=_cparams("parallel", "parallel"),
        name="ada_mods",
    )(cond, ada_w, ada_b.reshape(depth, 1, nd))
    return out.reshape(depth * MOD_ROWS * N_MODS, 1, d)


class _Dims:
    def __init__(self, x, ctx, ada_w, q_norm_g, sgu_w, ret_decay_fwd, ffn_w1, moe_w1):
        self.b, self.l, self.d = x.shape
        self.ctx = ctx.shape[1]
        self.depth = ada_w.shape[0]
        self.hd = q_norm_g.shape[1]
        self.mw = self.d // 2
        self.nq = self.mw // self.hd
        self.nkv = self.nq // GQA_GROUP
        self.kvw = self.nkv * self.hd
        self.sgu_groups, self.sgu_chunk = sgu_w.shape[1], sgu_w.shape[2]
        self.ret_heads = ret_decay_fwd.shape[1]
        self.ret_chunk = _largest_divisor(math.gcd(self.l, self.ctx), (256, 128))
        self.n_lat = self.b * self.l
        self.n_ctx = self.b * self.ctx
        self.n_tok = self.n_lat + self.n_ctx
        self.lk = self.ctx + self.l
        self.n_exp = moe_w1.shape[1]
        self.alpha = (2 * self.depth) ** 0.25
        mw = self.mw
        self.c_u, self.c_v = 0, mw
        self.c_rq, self.c_rk, self.c_rv, self.c_rg = 2 * mw, 3 * mw, 4 * mw, 5 * mw
        self.c_bg = 6 * mw
        self.c_aq = 6 * mw + 3 * self.d
        self.c_ak = self.c_aq + mw
        self.c_av = self.c_ak + self.kvw
        self.n_cols = self.c_av + self.kvw

    def mod_row(self, layer, j):
        def f(i, tm):
            bidx = jnp.minimum((i * tm) // self.l, self.b)
            return (layer * MOD_ROWS + bidx) * N_MODS + j
        return f


def _reorder_w_in(w, dm):
    mw, kvw = dm.mw, dm.kvw
    aq = w[:, :mw]
    ak = w[:, mw:mw + kvw]
    av = w[:, mw + kvw:mw + 2 * kvw]
    rest = w[:, mw + 2 * kvw:]
    return jnp.concatenate([rest, aq, ak, av], axis=1).astype(BF16)


def _proj_kernel(x_ref, sh_ref, sc_ref, w_ref, o_ref, h_ref):
    @pl.when(pl.program_id(1) == 0)
    def _():
        h = _ln(x_ref[...]) * (1.0 + sc_ref[...]) + sh_ref[...]
        h_ref[...] = h.astype(BF16)

    o_ref[...] = jnp.dot(h_ref[...], w_ref[...], preferred_element_type=F32).astype(o_ref.dtype)


def _in_projection(xs, mods, w, dm, layer):
    t, d = xs.shape
    nc = w.shape[1]
    tm = _largest_divisor(math.gcd(dm.l, dm.n_ctx), (1024, 512, 256, 128))
    tn = _largest_divisor(nc, (3 * dm.mw // 2, dm.mw // 2))
    row = dm.mod_row
    return pl.pallas_call(
        _proj_kernel,
        grid=(t // tm, nc // tn),
        in_specs=[
            pl.BlockSpec((tm, d), lambda i, j: (i, 0)),
            pl.BlockSpec((None, 1, d), lambda i, j: (row(layer, 0)(i, tm), 0, 0)),
            pl.BlockSpec((None, 1, d), lambda i, j: (row(layer, 1)(i, tm), 0, 0)),
            pl.BlockSpec((d, tn), lambda i, j: (0, j)),
        ],
        out_specs=pl.BlockSpec((tm, tn), lambda i, j: (i, j)),
        out_shape=jax.ShapeDtypeStruct((t, nc), BF16),
        scratch_shapes=[pltpu.VMEM((tm, d), BF16)],
        compiler_params=_cparams("parallel", "arbitrary"),
        name="in_projection",
    )(xs, mods, mods, w)


def _rope_tables(dm):
    n_freq = dm.hd // 4
    t = jnp.arange(dm.l, dtype=F32)
    row = jnp.floor(t / GRID_W)
    col = t - row * GRID_W
    inv_freq = ROPE_THETA ** (-jnp.arange(n_freq, dtype=F32) / n_freq)
    ar, ac = row[:, None] * inv_freq, col[:, None] * inv_freq
    z = jnp.zeros_like(ar)
    cos = jnp.concatenate([jnp.cos(ar), jnp.cos(ar), jnp.cos(ac), jnp.cos(ac)], axis=1)
    sin_lo = jnp.concatenate([z, jnp.sin(ar), z, jnp.sin(ac)], axis=1)
    sin_hi = jnp.concatenate([-jnp.sin(ar), z, -jnp.sin(ac), z], axis=1)
    pad = lambda a, v: jnp.concatenate([a, jnp.full((dm.ctx, dm.hd), v, F32)], axis=0)
    return pad(cos, 1.0), pad(sin_lo, 0.0), pad(sin_hi, 0.0)


def _prep_kernel(q_ref, k_ref, v_ref, qg_ref, kg_ref, cos_ref, slo_ref, shi_ref, qo_ref, kt_ref, vo_ref,
                 *, nq, nkv, hd, qscale):
    cos, slo, shi = cos_ref[...], slo_ref[...], shi_ref[...]
    quarter = hd // 4

    def norm_rope(xh, g):
        y = xh * lax.rsqrt(jnp.mean(xh * xh, axis=-1, keepdims=True) + EPS) * g
        return y * cos + pltpu.roll(y, quarter, 1) * slo + pltpu.roll(y, hd - quarter, 1) * shi

    qg = qg_ref[...] * qscale
    kg = kg_ref[...]
    for h in range(nq):
        sl = slice(h * hd, (h + 1) * hd)
        qo_ref[:, sl] = norm_rope(q_ref[:, sl].astype(F32), qg).astype(qo_ref.dtype)
    for g in range(nkv):
        sl = slice(g * hd, (g + 1) * hd)
        kt_ref[g] = norm_rope(k_ref[:, sl].astype(F32), kg).T.astype(kt_ref.dtype)
        vo_ref[g] = v_ref[:, sl]


def _attention_prep(proj, q_norm_g, k_norm_g, tables, dm):
    t = proj.shape[0]
    tp = _largest_divisor(math.gcd(dm.l, dm.ctx), (256, 128))
    mw, kvw, hd = dm.mw, dm.kvw, dm.hd
    n_lat_tiles = dm.n_lat // tp

    def batch_of(i):
        return jnp.where(i < n_lat_tiles, (i * tp) // dm.l, (i * tp - dm.n_lat) // dm.ctx)

    def key_block(i):
        lat = (dm.ctx + (i * tp) % dm.l) // tp
        ctx = ((i * tp - dm.n_lat) % dm.ctx) // tp
        return jnp.where(i < n_lat_tiles, lat, ctx)

    def table_block(i):
        lat = ((i * tp) % dm.l) // tp
        ctx = dm.l // tp + ((i * tp - dm.n_lat) % dm.ctx) // tp
        return jnp.where(i < n_lat_tiles, lat, ctx)

    tab_spec = pl.BlockSpec((tp, hd), lambda i: (table_block(i), 0))
    qscale = hd ** -0.5 * LOG2E
    return pl.pallas_call(
        functools.partial(_prep_kernel, nq=dm.nq, nkv=dm.nkv, hd=hd, qscale=qscale),
        grid=(t // tp,),
        in_specs=[
            pl.BlockSpec((tp, mw), lambda i: (i, dm.c_aq // mw)),
            pl.BlockSpec((tp, kvw), lambda i: (i, dm.c_ak // kvw)),
            pl.BlockSpec((tp, kvw), lambda i: (i, dm.c_av // kvw)),
            pl.BlockSpec((1, hd), lambda i: (0, 0)),
            pl.BlockSpec((1, hd), lambda i: (0, 0)),
            tab_spec, tab_spec, tab_spec,
        ],
        out_specs=[
            pl.BlockSpec((tp, mw), lambda i: (i, 0)),
            pl.BlockSpec((None, dm.nkv, hd, tp), lambda i: (batch_of(i), 0, 0, key_block(i))),
            pl.BlockSpec((None, dm.nkv, tp, hd), lambda i: (batch_of(i), 0, key_block(i), 0)),
        ],
        out_shape=[
            jax.ShapeDtypeStruct((t, mw), BF16),
            jax.ShapeDtypeStruct((dm.b, dm.nkv, hd, dm.lk), BF16),
            jax.ShapeDtypeStruct((dm.b, dm.nkv, dm.lk, hd), BF16),
        ],
        compiler_params=_cparams("parallel"),
        name="attention_prep",
    )(proj, proj, proj, q_norm_g.reshape(1, hd), k_norm_g.reshape(1, hd), *tables)


def _attn_kernel(q_ref, kt_ref, v_ref, o_ref, m_ref, l_ref, acc_ref, kmax_ref, *,
                 hd, n_lat_tiles, lat_keys, ctx_keys):
    tq = q_ref.shape[0]
    i = pl.program_id(2)
    q = q_ref[...]
    qs = jnp.concatenate([q[:, h * hd:(h + 1) * hd] for h in range(GQA_GROUP)], axis=0)

    @pl.when(i == 0)
    def _():
        k = kt_ref[...].astype(F32)
        k2 = jnp.max(jnp.sum(k * k, axis=0, keepdims=True), axis=1, keepdims=True)
        kmax_ref[...] = jnp.broadcast_to(jnp.sqrt(k2), kmax_ref.shape)

    qf = qs.astype(F32)
    shift = jnp.sqrt(jnp.sum(qf * qf, axis=-1, keepdims=True)) * kmax_ref[0:1, 0:1]
    bounded = jnp.max(shift) <= MAX_SAFE_SHIFT

    def finish(l):
        o = acc_ref[...] * (1.0 / l)
        o_ref[...] = jnp.concatenate([o[h * tq:(h + 1) * tq] for h in range(GQA_GROUP)],
                                     axis=1).astype(o_ref.dtype)

    def attend_bounded(tk, nk):
        cb = _largest_divisor(tk, (256, 128))
        m_ref[...] = jnp.broadcast_to(shift, m_ref.shape)
        l_ref[...] = jnp.zeros(l_ref.shape, F32)
        acc_ref[...] = jnp.zeros(acc_ref.shape, F32)

        def body(kt, carry):
            off = kt * tk
            mm = m_ref[...]
            mm = jnp.concatenate([mm] * (cb // 128), axis=1)
            lsum = l_ref[...]
            parts = []
            for c in range(tk // cb):
                col = pl.multiple_of(off + c * cb, cb)
                p = jnp.exp2(jnp.dot(qs, kt_ref[:, pl.ds(col, cb)], preferred_element_type=F32) - mm)
                for u in range(cb // 128):
                    lsum = lsum + p[:, u * 128:(u + 1) * 128]
                parts.append(p.astype(BF16))
            l_ref[...] = lsum
            acc_ref[...] += jnp.dot(jnp.concatenate(parts, axis=1), v_ref[pl.ds(pl.multiple_of(off, tk), tk), :],
                                    preferred_element_type=F32)
            return carry

        lax.fori_loop(0, nk, body, 0)
        finish(jnp.sum(l_ref[...], axis=-1, keepdims=True))

    def attend_online(tk, nk):
        m_ref[...] = jnp.full(m_ref.shape, -jnp.inf, F32)
        l_ref[...] = jnp.zeros(l_ref.shape, F32)
        acc_ref[...] = jnp.zeros(acc_ref.shape, F32)

        def body(kt, carry):
            off = pl.multiple_of(kt * tk, tk)
            s = jnp.dot(qs, kt_ref[:, pl.ds(off, tk)], preferred_element_type=F32)
            m_prev = m_ref[:, 0:1]
            m_new = jnp.maximum(m_prev, jnp.max(s, axis=-1, keepdims=True))
            p = jnp.exp2(s - m_new)
            a = jnp.exp2(m_prev - m_new)
            l_ref[:, 0:1] = a * l_ref[:, 0:1] + jnp.sum(p, axis=-1, keepdims=True)
            acc_ref[...] = a * acc_ref[...] + jnp.dot(p.astype(BF16), v_ref[pl.ds(off, tk), :],
                                                      preferred_element_type=F32)
            m_ref[:, 0:1] = m_new
            return carry

        lax.fori_loop(0, nk, body, 0)
        finish(l_ref[:, 0:1])

    def attend(keys):
        @pl.when(bounded)
        def _():
            attend_bounded(*keys)

        @pl.when(jnp.logical_not(bounded))
        def _():
            attend_online(*keys)

    is_latent = i < n_lat_tiles

    @pl.when(is_latent)
    def _():
        attend(lat_keys)

    if ctx_keys is not None:
        @pl.when(jnp.logical_not(is_latent))
        def _():
            attend(ctx_keys)


def _attention(qp, kt, v, dm, *, with_ctx):
    hd, mw = dm.hd, dm.mw
    tq = _largest_divisor(math.gcd(dm.l, dm.ctx), (256, 128))
    key_tiling = lambda n: (lambda tk: (tk, n // tk))(_largest_divisor(n, (768, 512, 256, 128)))
    gw = GQA_GROUP * hd
    nl, nc = dm.l // tq, dm.ctx // tq
    rows = dm.n_tok if with_ctx else dm.n_lat

    def q_map(b, g, i):
        return jnp.where(i < nl, b * nl + i, dm.n_lat // tq + b * nc + (i - nl)), g

    return pl.pallas_call(
        functools.partial(_attn_kernel, hd=hd, n_lat_tiles=nl, lat_keys=key_tiling(dm.lk),
                          ctx_keys=key_tiling(dm.ctx) if with_ctx else None),
        grid=(dm.b, dm.nkv, nl + (nc if with_ctx else 0)),
        in_specs=[
            pl.BlockSpec((tq, gw), q_map),
            pl.BlockSpec((None, None, hd, dm.lk), lambda b, g, i: (b, g, 0, 0)),
            pl.BlockSpec((None, None, dm.lk, hd), lambda b, g, i: (b, g, 0, 0)),
        ],
        out_specs=pl.BlockSpec((tq, gw), q_map),
        out_shape=jax.ShapeDtypeStruct((rows, mw), BF16),
        scratch_shapes=[
            pltpu.VMEM((GQA_GROUP * tq, 128), F32),
            pltpu.VMEM((GQA_GROUP * tq, 128), F32),
            pltpu.VMEM((GQA_GROUP * tq, hd), F32),
            pltpu.VMEM((8, 128), F32),
        ],
        compiler_params=_cparams("parallel", "parallel", "arbitrary"),
        name="attention",
    )(qp, kt, v)


def _sgu_kernel(u_ref, v_ref, g_ref, b_ref, w_ref, bs_ref, o_ref, *, ng, gw, chunk):
    ts = u_ref.shape[0]
    u = _gelu_tanh(u_ref[...].astype(F32))
    v = _ln(_gelu_tanh(v_ref[...].astype(F32))) * g_ref[...] + b_ref[...]
    vb = v.astype(BF16)
    for c in range(ts // chunk):
        rows = slice(c * chunk, (c + 1) * chunk)
        for g in range(ng):
            cols = slice(g * gw, (g + 1) * gw)
            z = jnp.dot(w_ref[g], vb[rows, cols], preferred_element_type=F32) + bs_ref[g]
            o_ref[rows, cols] = (u[rows, cols] * z).astype(o_ref.dtype)


def _sgu(proj, ln_g, ln_b, w_s, b_s, dm, rows):
    mw, chunk, ng = dm.mw, dm.sgu_chunk, dm.sgu_groups
    ts = _largest_divisor(math.gcd(dm.l, dm.n_ctx), (256, 128))
    return pl.pallas_call(
        functools.partial(_sgu_kernel, ng=ng, gw=mw // ng, chunk=chunk),
        grid=(rows // ts,),
        in_specs=[
            pl.BlockSpec((ts, mw), lambda i: (i, dm.c_u // mw)),
            pl.BlockSpec((ts, mw), lambda i: (i, dm.c_v // mw)),
            pl.BlockSpec((1, mw), lambda i: (0, 0)),
            pl.BlockSpec((1, mw), lambda i: (0, 0)),
            pl.BlockSpec((ng, chunk, chunk), lambda i: (0, 0, 0)),
            pl.BlockSpec((ng, chunk, 1), lambda i: (0, 0, 0)),
        ],
        out_specs=pl.BlockSpec((ts, mw), lambda i: (i, 0)),
        out_shape=jax.ShapeDtypeStruct((rows, mw), BF16),
        compiler_params=_cparams("parallel"),
        name="sgu",
    )(proj, proj, ln_g.reshape(1, mw), ln_b.reshape(1, mw), w_s.astype(BF16), b_s.reshape(ng, chunk, 1))


def _ret_tables(decay_logit, dm, backward):
    c, hd, nh = dm.ret_chunk, dm.hd, dm.ret_heads
    lg = jax.nn.log_sigmoid(decay_logit.astype(F32))
    pos = jnp.arange(c, dtype=F32)
    diff = pos[:, None] - pos[None, :]
    if backward:
        diff = -diff
        xi_e = c - pos
        zeta_e = pos
    else:
        xi_e = pos + 1.0
        zeta_e = c - 1.0 - pos
    dmat = jnp.where(diff >= 0, jnp.exp(jnp.maximum(diff, 0.0)[None] * lg[:, None, None]), 0.0)
    dmat = dmat * hd ** -0.5
    xi = jnp.exp(xi_e[None, :] * lg[:, None])[..., None]
    zeta = (jnp.exp(zeta_e[None, :] * lg[:, None]) * hd ** -0.5)[..., None]
    dec = jnp.broadcast_to(jnp.exp(c * lg)[:, None, None], (nh, 1, hd))
    return dmat.astype(F32), xi.astype(F32), zeta.astype(F32), dec.astype(F32)


def _ret_kernel(*refs, nh, hd, final):
    if final:
        q_ref, k_ref, v_ref, dm_ref, xi_ref, zeta_ref, dec_ref, of_ref, g_ref, o_ref, st_ref = refs
    else:
        q_ref, k_ref, v_ref, dm_ref, xi_ref, zeta_ref, dec_ref, o_ref, st_ref = refs

    @pl.when(pl.program_id(1) == 0)
    def _():
        st_ref[...] = jnp.zeros(st_ref.shape, F32)

    for h in range(nh):
        sl = slice(h * hd, (h + 1) * hd)
        qh, kh, vh = q_ref[:, sl], k_ref[:, sl], v_ref[:, sl]
        s = lax.dot_general(qh, kh, (((1,), (1,)), ((), ())), preferred_element_type=F32)
        inner = (s * dm_ref[h]).astype(BF16)
        st = st_ref[h]
        o = jnp.dot(inner, vh, preferred_element_type=F32)
        o = o + jnp.dot(qh, st.astype(BF16), preferred_element_type=F32) * xi_ref[h]
        kz = (kh.astype(F32) * zeta_ref[h]).astype(BF16)
        st_ref[h] = st * dec_ref[h] + lax.dot_general(kz, vh, (((0,), (0,)), ((), ())),
                                                      preferred_element_type=F32)
        if final:
            gn = _ln(of_ref[:, sl] + o)
            o_ref[:, sl] = (_silu(g_ref[:, sl].astype(F32)) * gn).astype(o_ref.dtype)
        else:
            o_ref[:, sl] = o


def _retention_pass(proj, tables, dm, *, backward, o_fwd=None):
    c, mw, nh, hd = dm.ret_chunk, dm.mw, dm.ret_heads, dm.hd
    n_cc, n_lc = dm.ctx // c, dm.l // c
    steps = n_cc + n_lc

    def row_block(b, t):
        if backward:
            ctx = dm.n_lat // c + b * n_cc + (n_cc - 1 - t)
            lat = b * n_lc + (n_lc - 1 - (t - n_cc))
        else:
            ctx = dm.n_lat // c + b * n_cc + t
            lat = b * n_lc + (t - n_cc)
        return jnp.where(t < n_cc, ctx, lat)

    col = lambda off: (lambda b, t: (row_block(b, t), off // mw))
    const3 = lambda b, t: (0, 0, 0)
    in_specs = [
        pl.BlockSpec((c, mw), col(dm.c_rq)),
        pl.BlockSpec((c, mw), col(dm.c_rk)),
        pl.BlockSpec((c, mw), col(dm.c_rv)),
        pl.BlockSpec((nh, c, c), const3),
        pl.BlockSpec((nh, c, 1), const3),
        pl.BlockSpec((nh, c, 1), const3),
        pl.BlockSpec((nh, 1, hd), const3),
    ]
    args = [proj, proj, proj, *tables]
    final = o_fwd is not None
    if final:
        in_specs += [pl.BlockSpec((c, mw), col(0)), pl.BlockSpec((c, mw), col(dm.c_rg))]
        args += [o_fwd, proj]
    return pl.pallas_call(
        functools.partial(_ret_kernel, nh=nh, hd=hd, final=final),
        grid=(dm.b, steps),
        in_specs=in_specs,
        out_specs=pl.BlockSpec((c, mw), col(0)),
        out_shape=jax.ShapeDtypeStruct((dm.n_tok, mw), BF16 if final else F32),
        scratch_shapes=[pltpu.VMEM((nh, hd, hd), F32)],
        compiler_params=_cparams("parallel", "arbitrary"),
        name="retention_bwd" if final else "retention_fwd",
    )(*args)


def _merge_kernel(att_ref, sgu_ref, ret_ref, ga_ref, gs_ref, gr_ref, wa_ref, ws_ref, wr_ref, wo_ref,
                  x_ref, g1_ref, lng_ref, lnb_ref, o_ref, *, alpha):
    def branch(a_ref, w_ref, gate_ref):
        return jax.nn.sigmoid(gate_ref[...].astype(F32)) * jnp.dot(a_ref[...], w_ref[...],
                                                                    preferred_element_type=F32)

    m = branch(att_ref, wa_ref, ga_ref) + branch(sgu_ref, ws_ref, gs_ref) + branch(ret_ref, wr_ref, gr_ref)
    mix = jnp.dot(m.astype(BF16), wo_ref[...], preferred_element_type=F32)
    y = alpha * x_ref[...] + g1_ref[...] * mix
    o_ref[...] = _ln(y) * lng_ref[...] + lnb_ref[...]


def _merge(att, sgu, ret, proj, w_att, w_sgu, w_ret, w_o, xs, mods, ln_g, ln_b, dm, layer, rows):
    d, mw = dm.d, dm.mw
    tm = _largest_divisor(math.gcd(dm.l, dm.n_ctx), (256, 128))
    row = dm.mod_row
    tok = lambda w: pl.BlockSpec((tm, w), lambda i: (i, 0))
    gate = lambda k: pl.BlockSpec((tm, d), lambda i: (i, dm.c_bg // d + k))
    vec = pl.BlockSpec((1, d), lambda i: (0, 0))
    return pl.pallas_call(
        functools.partial(_merge_kernel, alpha=dm.alpha),
        grid=(rows // tm,),
        in_specs=[
            tok(mw), tok(mw), tok(mw), gate(0), gate(1), gate(2),
            _resident((mw, d), lambda i: (0, 0)), _resident((mw, d), lambda i: (0, 0)),
            _resident((mw, d), lambda i: (0, 0)), _resident((d, d), lambda i: (0, 0)),
            tok(d),
            pl.BlockSpec((None, 1, d), lambda i: (row(layer, 2)(i, tm), 0, 0)),
            vec, vec,
        ],
        out_specs=tok(d),
        out_shape=jax.ShapeDtypeStruct((rows, d), F32),
        compiler_params=_cparams("parallel"),
        name="merge",
    )(att, sgu, ret, proj, proj, proj, w_att, w_sgu, w_ret, w_o, xs, mods,
      ln_g.reshape(1, d), ln_b.reshape(1, d))


def _ffn_kernel(x_ref, sh_ref, sc_ref, w1_ref, w3_ref, w2_ref, g2_ref, lng_ref, lnb_ref, o_ref,
                h_ref, acc_ref, *, alpha):
    j = pl.program_id(1)

    @pl.when(j == 0)
    def _():
        h = _ln(x_ref[...]) * (1.0 + sc_ref[...]) + sh_ref[...]
        h_ref[...] = h.astype(BF16)
        acc_ref[...] = jnp.zeros(acc_ref.shape, F32)

    h = h_ref[...]
    a = jnp.dot(h, w1_ref[...], preferred_element_type=F32)
    b = jnp.dot(h, w3_ref[...], preferred_element_type=F32)
    acc_ref[...] += jnp.dot((_silu(a) * b).astype(BF16), w2_ref[...], preferred_element_type=F32)

    @pl.when(j == pl.num_programs(1) - 1)
    def _():
        y = alpha * x_ref[...] + g2_ref[...] * acc_ref[...]
        o_ref[...] = _ln(y) * lng_ref[...] + lnb_ref[...]


def _ffn(xs, mods, w1, w3, w2, ln_g, ln_b, dm, layer):
    t, d = xs.shape
    f = w1.shape[1]
    tm = _largest_divisor(math.gcd(dm.l, dm.n_ctx), (512, 256, 128))
    tf = _largest_divisor(f, (512, 256, 128))
    row = dm.mod_row
    mod = lambda k: pl.BlockSpec((None, 1, d), lambda i, j: (row(layer, k)(i, tm), 0, 0))
    vec = pl.BlockSpec((1, d), lambda i, j: (0, 0))
    return pl.pallas_call(
        functools.partial(_ffn_kernel, alpha=dm.alpha),
        grid=(t // tm, f // tf),
        in_specs=[
            pl.BlockSpec((tm, d), lambda i, j: (i, 0)),
            mod(3), mod(4),
            pl.BlockSpec((d, tf), lambda i, j: (0, j)),
            pl.BlockSpec((d, tf), lambda i, j: (0, j)),
            pl.BlockSpec((tf, d), lambda i, j: (j, 0)),
            mod(5), vec, vec,
        ],
        out_specs=pl.BlockSpec((tm, d), lambda i, j: (i, 0)),
        out_shape=jax.ShapeDtypeStruct((t, d), F32),
        scratch_shapes=[pltpu.VMEM((tm, d), BF16), pltpu.VMEM((tm, d), F32)],
        compiler_params=_cparams("parallel", "arbitrary"),
        name="ffn_dense",
    )(xs, mods, mods, w1, w3, w2, mods, ln_g.reshape(1, d), ln_b.reshape(1, d))


def _router_kernel(x_ref, sh_ref, sc_ref, whi_ref, wlo_ref, tril_ref, h_ref, gates_ref, code_ref, rank_ref,
                   count_ref):
    h = _ln(x_ref[...]) * (1.0 + sc_ref[...]) + sh_ref[...]
    _store_token_major(h_ref, h)
    h_hi = h.astype(BF16)
    h_lo = (h - h_hi.astype(F32)).astype(BF16)
    w_hi = whi_ref[...]
    logits = (jnp.dot(h_hi, w_hi, preferred_element_type=F32)
              + jnp.dot(h_lo, w_hi, preferred_element_type=F32)
              + jnp.dot(h_hi, wlo_ref[...], preferred_element_type=F32))
    n_exp = float(logits.shape[-1])
    ids = lax.broadcasted_iota(jnp.int32, logits.shape, 1).astype(F32)
    m1 = jnp.max(logits, axis=-1, keepdims=True)
    i1 = jnp.min(jnp.where(logits == m1, ids, n_exp), axis=-1, keepdims=True)
    rest = jnp.where(ids == i1, -jnp.inf, logits)
    m2 = jnp.max(rest, axis=-1, keepdims=True)
    i2 = jnp.min(jnp.where(rest == m2, ids, n_exp), axis=-1, keepdims=True)
    e = jnp.exp(m2 - m1)
    w1 = 1.0 / (1.0 + e)
    w2 = e * w1
    gates_ref[...] = jnp.where(ids == i1, w1, jnp.where(ids == i2, w2, 0.0))
    code_ref[...] = jnp.where(ids == i1, 1, jnp.where(ids == i2, 2, 0)).astype(jnp.int32)

    @pl.when(pl.program_id(0) == 0)
    def _():
        count_ref[...] = jnp.zeros(count_ref.shape, F32)

    chosen = jnp.where(jnp.logical_or(ids == i1, ids == i2), 1.0, 0.0)
    before = jnp.dot(tril_ref[...], chosen.astype(BF16), preferred_element_type=F32) + count_ref[...]
    rank_ref[...] = before.astype(jnp.int32)
    count_ref[...] = count_ref[...] + jnp.sum(chosen, axis=0, keepdims=True)


def _router(xs, mods, router_w, dm, layer, rows):
    d, n_exp = router_w.shape
    tm = _largest_divisor(dm.l, (512, 256, 128))
    row = dm.mod_row
    mod = lambda k: pl.BlockSpec((None, 1, d), lambda i: (row(layer, k)(i, tm), 0, 0))
    w_hi = router_w.astype(BF16)
    w_lo = (router_w - w_hi.astype(F32)).astype(BF16)
    return pl.pallas_call(
        _router_kernel,
        grid=(rows // tm,),
        in_specs=[
            pl.BlockSpec((tm, d), lambda i: (i, 0)), mod(3), mod(4),
            pl.BlockSpec((d, n_exp), lambda i: (0, 0)),
            pl.BlockSpec((d, n_exp), lambda i: (0, 0)),
            pl.BlockSpec((tm, tm), lambda i: (0, 0)),
        ],
        out_specs=[
            pl.BlockSpec((tm * (d // LANES), LANES), lambda i: (i, 0)),
            pl.BlockSpec((tm, n_exp), lambda i: (i, 0)),
            pl.BlockSpec((tm, n_exp), lambda i: (i, 0)),
            pl.BlockSpec((tm, n_exp), lambda i: (i, 0)),
        ],
        out_shape=[
            jax.ShapeDtypeStruct((rows * (d // LANES), LANES), F32),
            jax.ShapeDtypeStruct((rows, n_exp), F32),
            jax.ShapeDtypeStruct((rows, n_exp), jnp.int32),
            jax.ShapeDtypeStruct((rows, n_exp), jnp.int32),
        ],
        scratch_shapes=[pltpu.VMEM((1, n_exp), F32)],
        compiler_params=_cparams("arbitrary"),
        name="router",
    )(xs, mods, mods, w_hi, w_lo, jnp.tril(jnp.ones((tm, tm), BF16), -1))


def _token_rows(ref, t, g):
    return ref.at[pl.ds(pl.multiple_of(t * g, g), g), :]


def _row_gather(index_of, lo, hi, hbm_ref, buf_ref, sem, g, *, wait):
    def body(q, carry):
        for u in range(ROW_DMA_UNROLL):
            r = lo + q * ROW_DMA_UNROLL + u
            cp = pltpu.make_async_copy(_token_rows(hbm_ref, index_of(r), g), _token_rows(buf_ref, r, g), sem)
            if wait:
                cp.wait()
            else:
                cp.start(priority=0)
        return carry

    lax.fori_loop(0, (hi - lo) // ROW_DMA_UNROLL, body, 0)


def _row_scatter(index_of, lo, hi, buf_ref, hbm_ref, sem, g, *, wait):
    def body(r, carry):
        cp = pltpu.make_async_copy(_token_rows(buf_ref, r, g), _token_rows(hbm_ref, index_of(r), g), sem)
        if wait:
            cp.wait()
        else:
            cp.start(priority=1)
        return carry

    lax.fori_loop(lo, hi, body, 0)


def _expert_kernel(te_ref, tv_ref, tok_cur, tok_nxt, dst_prv, h_hbm, w1_ref, w3_ref, w2_ref, y_hbm,
                   in_buf, out_buf, in_sem, out_sem, h_ref, acc_ref, *, nj):
    i, j = pl.program_id(0), pl.program_id(1)
    nt = pl.num_programs(0)
    tm = h_ref.shape[0]
    g = h_ref.shape[1] // LANES
    slot = i % 2
    chunk = -(-tm // (nj * ROW_DMA_UNROLL)) * ROW_DMA_UNROLL
    lo, hi = jnp.minimum(j * chunk, tm), jnp.minimum((j + 1) * chunk, tm)
    valid = tv_ref[i] > 0
    prev_rows = jnp.where(i > 0, tv_ref[jnp.maximum(i - 1, 0)], 0)
    prev_valid = prev_rows > 0
    next_valid = jnp.logical_and(i + 1 < nt, tv_ref[jnp.minimum(i + 1, nt - 1)] > 0)
    gather = functools.partial(_row_gather, hbm_ref=h_hbm, g=g)
    scatter = functools.partial(_row_scatter, lambda r: dst_prv[0, 0, r], buf_ref=out_buf.at[1 - slot],
                                hbm_ref=y_hbm, sem=out_sem.at[1 - slot], g=g)

    @pl.when(jnp.logical_and(valid, j == 0))
    def _():
        @pl.when(i == 0)
        def _():
            gather(lambda r: tok_cur[0, 0, r], 0, tm, buf_ref=in_buf.at[0], sem=in_sem.at[0], wait=False)

        gather(lambda r: tok_cur[0, 0, r], 0, tm, buf_ref=in_buf.at[slot], sem=in_sem.at[slot], wait=True)
        h_ref[...] = _load_token_major(in_buf.at[slot], tm, g).astype(BF16)
        acc_ref[...] = jnp.zeros(acc_ref.shape, F32)

    @pl.when(next_valid)
    def _():
        gather(lambda r: tok_nxt[0, 0, r], lo, hi, buf_ref=in_buf.at[1 - slot], sem=in_sem.at[1 - slot], wait=False)

    @pl.when(prev_valid)
    def _():
        scatter(jnp.minimum(lo, prev_rows), jnp.minimum(hi, prev_rows), wait=False)

    @pl.when(valid)
    def _():
        h = h_ref[...]
        a = jnp.dot(h, w1_ref[...], preferred_element_type=F32)
        b = jnp.dot(h, w3_ref[...], preferred_element_type=F32)
        acc_ref[...] += jnp.dot((_silu(a) * b).astype(BF16), w2_ref[...], preferred_element_type=F32)

        @pl.when(j == nj - 1)
        def _():
            _store_token_major(out_buf.at[slot], acc_ref[...])

    @pl.when(jnp.logical_and(prev_valid, j == nj - 1))
    def _():
        scatter(0, prev_rows, wait=True)


def _expert_mlp(h_tm, slot_token, slot_dest, tile_expert, tile_valid, w1, w3, w2, tm, d, n_out):
    n_slots = slot_token.shape[0]
    f = w1.shape[2]
    g = d // LANES
    tf = _largest_divisor(f, (1024, 512, 256, 128))
    nj, nt = f // tf, n_slots // tm
    jj = lambda i, j, tv: jnp.where(tv[i] > 0, j, nj - 1)
    tok = slot_token.reshape(nt, 1, tm)
    dst = slot_dest.reshape(nt, 1, tm)
    smem = lambda f_i: pl.BlockSpec((1, 1, tm), lambda i, j, te, tv: (f_i(i), 0, 0), memory_space=pltpu.SMEM)
    grid_spec = pltpu.PrefetchScalarGridSpec(
        num_scalar_prefetch=2,
        grid=(nt, nj),
        in_specs=[
            smem(lambda i: i),
            smem(lambda i: jnp.minimum(i + 1, nt - 1)),
            smem(lambda i: jnp.maximum(i - 1, 0)),
            pl.BlockSpec(memory_space=pl.ANY),
            pl.BlockSpec((None, d, tf), lambda i, j, te, tv: (te[i], 0, jj(i, j, tv))),
            pl.BlockSpec((None, d, tf), lambda i, j, te, tv: (te[i], 0, jj(i, j, tv))),
            pl.BlockSpec((None, tf, d), lambda i, j, te, tv: (te[i], jj(i, j, tv), 0)),
        ],
        out_specs=pl.BlockSpec(memory_space=pl.ANY),
        scratch_shapes=[
            pltpu.VMEM((2, tm * g, LANES), F32),
            pltpu.VMEM((2, tm * g, LANES), F32),
            pltpu.SemaphoreType.DMA((2,)),
            pltpu.SemaphoreType.DMA((2,)),
            pltpu.VMEM((tm, d), BF16),
            pltpu.VMEM((tm, d), F32),
        ],
    )
    return pl.pallas_call(
        functools.partial(_expert_kernel, nj=nj),
        grid_spec=grid_spec,
        out_shape=jax.ShapeDtypeStruct((n_out * g, LANES), F32),
        compiler_params=_cparams("arbitrary", "arbitrary"),
        name="expert_mlp",
    )(tile_expert, tile_valid, tok, tok, dst, h_tm, w1, w3, w2)


def _moe_out_kernel(y1_ref, y2_ref, gates_ref, code_ref, x_ref, g2_ref, lng_ref, lnb_ref, o_ref, *, alpha):
    tm, d = x_ref.shape
    g = d // LANES
    gates, code = gates_ref[...], code_ref[...]
    f = jnp.zeros((tm, d), F32)
    for k, y_ref in enumerate((y1_ref, y2_ref)):
        w = jnp.sum(jnp.where(code == k + 1, gates, 0.0), axis=-1, keepdims=True)
        f = f + w * _load_token_major(y_ref, tm, g)
    y = alpha * x_ref[...] + g2_ref[...] * f
    o_ref[...] = _ln(y) * lng_ref[...] + lnb_ref[...]


def _moe_out(y_tm, gates, code, xs, mods, ln_g, ln_b, dm, layer, rows):
    d, n_exp = dm.d, gates.shape[1]
    g = d // LANES
    tm = _largest_divisor(dm.l, (256, 128))
    nt = rows // tm
    row = dm.mod_row
    vec = pl.BlockSpec((1, d), lambda i: (0, 0))
    return pl.pallas_call(
        functools.partial(_moe_out_kernel, alpha=dm.alpha),
        grid=(nt,),
        in_specs=[
            pl.BlockSpec((tm * g, LANES), lambda i: (i, 0)),
            pl.BlockSpec((tm * g, LANES), lambda i: (nt + i, 0)),
            pl.BlockSpec((tm, n_exp), lambda i: (i, 0)),
            pl.BlockSpec((tm, n_exp), lambda i: (i, 0)),
            pl.BlockSpec((tm, d), lambda i: (i, 0)),
            pl.BlockSpec((None, 1, d), lambda i: (row(layer, 5)(i, tm), 0, 0)),
            vec, vec,
        ],
        out_specs=pl.BlockSpec((tm, d), lambda i: (i, 0)),
        out_shape=jax.ShapeDtypeStruct((rows, d), F32),
        compiler_params=_cparams("parallel"),
        name="moe_out",
    )(y_tm, y_tm, gates, code, xs, mods, ln_g.reshape(1, d), ln_b.reshape(1, d))


def _moe(xs, mods, router_w, w1, w3, w2, ln_g, ln_b, dm, layer, rows):
    n_exp = router_w.shape[1]
    tm = 512 if rows * TOP_K >= 512 * n_exp * 4 else 128
    h, gates, code, rank = _router(xs, mods, router_w, dm, layer, rows)

    counts = jnp.sum((code > 0).astype(jnp.int32), axis=0)
    padded = ((counts + tm - 1) // tm) * tm
    ends = jnp.cumsum(padded)
    starts = ends - padded
    slot_of = starts[None, :] + rank
    slot = jnp.concatenate([jnp.sum(jnp.where(code == k + 1, slot_of, 0), axis=1) for k in range(TOP_K)])
    n_slots = TOP_K * rows + (n_exp + 1) * tm
    assignment = jnp.arange(TOP_K * rows, dtype=jnp.int32)
    slot_dest = jnp.full((n_slots,), -1, jnp.int32).at[slot].set(assignment)
    slot_token = jnp.maximum(slot_dest, 0) % rows
    tile_start = jnp.arange(n_slots // tm, dtype=jnp.int32) * tm
    last_used = jnp.minimum(tile_start, ends[-1] - 1)
    tile_expert = jnp.minimum(jnp.sum((ends[None, :] <= last_used[:, None]).astype(jnp.int32), axis=1),
                              n_exp - 1).astype(jnp.int32)
    tile_rows = jnp.where(tile_start < ends[-1],
                          jnp.clip((starts + counts)[tile_expert] - tile_start, 0, tm), 0).astype(jnp.int32)

    y = _expert_mlp(h, slot_token, slot_dest, tile_expert, tile_rows, w1, w3, w2, tm, dm.d, TOP_K * rows)
    return _moe_out(y, gates, code, xs, mods, ln_g, ln_b, dm, layer, rows)


def kernel(x, c, ctx, c_ctx, ada_w, ada_b, w_in, q_norm_g, k_norm_g, sgu_ln_g, sgu_ln_b, sgu_w, sgu_b,
           ret_decay_fwd, ret_decay_bwd, w_br_att, w_br_sgu, w_br_ret, w_out, ln1_g, ln1_b, ln2_g, ln2_b,
           ffn_w1, ffn_w3, ffn_w2, router_w, moe_w1, moe_w3, moe_w2):
    dm = _Dims(x, ctx, ada_w, q_norm_g, sgu_w, ret_decay_fwd, ffn_w1, moe_w1)
    assert dm.b + 1 <= MOD_ROWS and dm.ret_heads == dm.nq
    d = dm.d

    cond = jnp.concatenate([c, c_ctx[None], jnp.zeros((MOD_ROWS - dm.b - 1, d), F32)], axis=0)
    mods = _ada_mods(cond, ada_w, ada_b)
    rope = _rope_tables(dm)

    xs = jnp.concatenate([x.reshape(dm.n_lat, d), ctx.reshape(dm.n_ctx, d)], axis=0)
    for i in range(dm.depth):
        last = i == dm.depth - 1
        rows = dm.n_lat if last else dm.n_tok

        proj = _in_projection(xs, mods, _reorder_w_in(w_in[i], dm), dm, i)

        qp, kt, v_all = _attention_prep(proj, q_norm_g[i], k_norm_g[i], rope, dm)
        att = _attention(qp, kt, v_all, dm, with_ctx=not last)

        sgu = _sgu(proj, sgu_ln_g[i], sgu_ln_b[i], sgu_w[i], sgu_b[i], dm, rows)

        o_fwd = _retention_pass(proj, _ret_tables(ret_decay_fwd[i], dm, False), dm, backward=False)
        ret = _retention_pass(proj, _ret_tables(ret_decay_bwd[i], dm, True), dm, backward=True, o_fwd=o_fwd)

        xs1 = _merge(att, sgu, ret, proj, w_br_att[i].astype(BF16), w_br_sgu[i].astype(BF16),
                     w_br_ret[i].astype(BF16), w_out[i].astype(BF16), xs, mods, ln1_g[i], ln1_b[i], dm, i, rows)

        j = i // 2
        if i % 2 == 0:
            assert rows == xs1.shape[0]
            xs = _ffn(xs1, mods, ffn_w1[j].astype(BF16), ffn_w3[j].astype(BF16), ffn_w2[j].astype(BF16),
                      ln2_g[i], ln2_b[i], dm, i)
        else:
            xs = _moe(xs1, mods, router_w[j], moe_w1[j].astype(BF16), moe_w3[j].astype(BF16),
                      moe_w2[j].astype(BF16), ln2_g[i], ln2_b[i], dm, i, rows)
    return xs[:dm.n_lat].reshape(dm.b, dm.l, d)
```

```python
import functools
import math

import jax
import jax.numpy as jnp
from jax import lax
from jax.experimental import pallas as pl
from jax.experimental.pallas import tpu as pltpu

F32 = jnp.float32
BF16 = jnp.bfloat16
EPS = 1e-6
GRID_W = 64
ROPE_THETA = 10000.0
GQA_GROUP = 4
TOP_K = 2
N_MODS = 6
MOD_ROWS = 8
LANES = 128
ROW_DMA_UNROLL = 2
V7X_VMEM_BYTES = 64 * 1024 * 1024
VMEM_LIMIT = V7X_VMEM_BYTES - 8 * 1024 * 1024
LOG2E = 1.4426950408889634
MAX_SAFE_SHIFT = 60.0


def _cparams(*sem):
    return pltpu.CompilerParams(dimension_semantics=sem, vmem_limit_bytes=VMEM_LIMIT)


def _resident(shape, index_map):
    return pl.BlockSpec(shape, index_map, pipeline_mode=pl.Buffered(1))


def _largest_divisor(n, candidates):
    for c in candidates:
        if n % c == 0:
            return c
    raise ValueError(f"no tile in {candidates} divides {n}")


def _ln(x):
    mu = jnp.mean(x, axis=-1, keepdims=True)
    xc = x - mu
    var = jnp.mean(xc * xc, axis=-1, keepdims=True)
    return xc * lax.rsqrt(var + EPS)


def _silu(x):
    return x * jax.nn.sigmoid(x)


def _gelu_tanh(x):
    return 0.5 * x * (1.0 + jnp.tanh(math.sqrt(2.0 / math.pi) * (x + 0.044715 * (x * x * x))))


def _store_token_major(ref, x):
    t, d = x.shape
    g = d // LANES
    for c in range(g):
        ref[pl.ds(c, t, stride=g), :] = x[:, c * LANES:(c + 1) * LANES]


def _load_token_major(ref, t, g):
    return jnp.concatenate([ref[pl.ds(c, t, stride=g), :] for c in range(g)], axis=1)


def _ada_kernel(c_ref, w_ref, b_ref, o_ref):
    s = _silu(c_ref[...]).astype(BF16)
    o_ref[...] = jnp.dot(s, w_ref[...].astype(BF16), preferred_element_type=F32) + b_ref[...]


def _ada_mods(cond, ada_w, ada_b):
    depth, d, nd = ada_w.shape
    tn = _largest_divisor(nd, (1024, 512, 256, 128))
    out = pl.pallas_call(
        _ada_kernel,
        grid=(depth, nd // tn),
        in_specs=[
            pl.BlockSpec((MOD_ROWS, d), lambda l, j: (0, 0)),
            pl.BlockSpec((None, d, tn), lambda l, j: (l, 0, j)),
            pl.BlockSpec((None, 1, tn), lambda l, j: (l, 0, j)),
        ],
        out_specs=pl.BlockSpec((None, MOD_ROWS, tn), lambda l, j: (l, 0, j)),
        out_shape=jax.ShapeDtypeStruct((depth, MOD_ROWS, nd), F32),
        compiler_params=_cparams("parallel", "parallel"),
        name="ada_mods",
    )(cond, ada_w, ada_b.reshape(depth, 1, nd))
    return out.reshape(depth * MOD_ROWS * N_MODS, 1, d)


class _Dims:
    def __init__(self, x, ctx, ada_w, q_norm_g, sgu_w, ret_decay_fwd, ffn_w1, moe_w1):
        self.b, self.l, self.d = x.shape
        self.ctx = ctx.shape[1]
        self.depth = ada_w.shape[0]
        self.hd = q_norm_g.shape[1]
        self.mw = self.d // 2
        self.nq = self.mw // self.hd
        self.nkv = self.nq // GQA_GROUP
        self.kvw = self.nkv * self.hd
        self.sgu_groups, self.sgu_chunk = sgu_w.shape[1], sgu_w.shape[2]
        self.ret_heads = ret_decay_fwd.shape[1]
        self.ret_chunk = _largest_divisor(math.gcd(self.l, self.ctx), (256, 128))
        self.n_lat = self.b * self.l
        self.n_ctx = self.b * self.ctx
        self.n_tok = self.n_lat + self.n_ctx
        self.lk = self.ctx + self.l
        self.n_exp = moe_w1.shape[1]
        self.alpha = (2 * self.depth) ** 0.25
        mw = self.mw
        self.c_u, self.c_v = 0, mw
        self.c_rq, self.c_rk, self.c_rv, self.c_rg = 2 * mw, 3 * mw, 4 * mw, 5 * mw
        self.c_bg = 6 * mw
        self.c_aq = 6 * mw + 3 * self.d
        self.c_ak = self.c_aq + mw
        self.c_av = self.c_ak + self.kvw
        self.n_cols = self.c_av + self.kvw

    def mod_row(self, layer, j):
        def f(i, tm):
            bidx = jnp.minimum((i * tm) // self.l, self.b)
            return (layer * MOD_ROWS + bidx) * N_MODS + j
        return f


def _reorder_w_in(w, dm):
    mw, kvw = dm.mw, dm.kvw
    aq = w[:, :mw]
    ak = w[:, mw:mw + kvw]
    av = w[:, mw + kvw:mw + 2 * kvw]
    rest = w[:, mw + 2 * kvw:]
    return jnp.concatenate([rest, aq, ak, av], axis=1).astype(BF16)


def _proj_kernel(x_ref, sh_ref, sc_ref, w_ref, o_ref, h_ref):
    @pl.when(pl.program_id(1) == 0)
    def _():
        h = _ln(x_ref[...]) * (1.0 + sc_ref[...]) + sh_ref[...]
        h_ref[...] = h.astype(BF16)

    o_ref[...] = jnp.dot(h_ref[...], w_ref[...], preferred_element_type=F32).astype(o_ref.dtype)


def _in_projection(xs, mods, w, dm, layer):
    t, d = xs.shape
    nc = w.shape[1]
    tm = _largest_divisor(math.gcd(dm.l, dm.n_ctx), (1024, 512, 256, 128))
    tn = _largest_divisor(nc, (3 * dm.mw // 2, dm.mw // 2))
    row = dm.mod_row
    return pl.pallas_call(
        _proj_kernel,
        grid=(t // tm, nc // tn),
        in_specs=[
            pl.BlockSpec((tm, d), lambda i, j: (i, 0)),
            pl.BlockSpec((None, 1, d), lambda i, j: (row(layer, 0)(i, tm), 0, 0)),
            pl.BlockSpec((None, 1, d), lambda i, j: (row(layer, 1)(i, tm), 0, 0)),
            pl.BlockSpec((d, tn), lambda i, j: (0, j)),
        ],
        out_specs=pl.BlockSpec((tm, tn), lambda i, j: (i, j)),
        out_shape=jax.ShapeDtypeStruct((t, nc), BF16),
        scratch_shapes=[pltpu.VMEM((tm, d), BF16)],
        compiler_params=_cparams("parallel", "arbitrary"),
        name="in_projection",
    )(xs, mods, mods, w)


def _rope_tables(dm):
    n_freq = dm.hd // 4
    t = jnp.arange(dm.l, dtype=F32)
    row = jnp.floor(t / GRID_W)
    col = t - row * GRID_W
    inv_freq = ROPE_THETA ** (-jnp.arange(n_freq, dtype=F32) / n_freq)
    ar, ac = row[:, None] * inv_freq, col[:, None] * inv_freq
    z = jnp.zeros_like(ar)
    cos = jnp.concatenate([jnp.cos(ar), jnp.cos(ar), jnp.cos(ac), jnp.cos(ac)], axis=1)
    sin_lo = jnp.concatenate([z, jnp.sin(ar), z, jnp.sin(ac)], axis=1)
    sin_hi = jnp.concatenate([-jnp.sin(ar), z, -jnp.sin(ac), z], axis=1)
    pad = lambda a, v: jnp.concatenate([a, jnp.full((dm.ctx, dm.hd), v, F32)], axis=0)
    return pad(cos, 1.0), pad(sin_lo, 0.0), pad(sin_hi, 0.0)


def _prep_kernel(q_ref, k_ref, v_ref, qg_ref, kg_ref, cos_ref, slo_ref, shi_ref, qo_ref, kt_ref, vo_ref,
                 *, nq, nkv, hd, qscale):
    cos, slo, shi = cos_ref[...], slo_ref[...], shi_ref[...]
    quarter = hd // 4

    def norm_rope(xh, g):
        y = xh * lax.rsqrt(jnp.mean(xh * xh, axis=-1, keepdims=True) + EPS) * g
        return y * cos + pltpu.roll(y, quarter, 1) * slo + pltpu.roll(y, hd - quarter, 1) * shi

    qg = qg_ref[...] * qscale
    kg = kg_ref[...]
    for h in range(nq):
        sl = slice(h * hd, (h + 1) * hd)
        qo_ref[:, sl] = norm_rope(q_ref[:, sl].astype(F32), qg).astype(qo_ref.dtype)
    for g in range(nkv):
        sl = slice(g * hd, (g + 1) * hd)
        kt_ref[g] = norm_rope(k_ref[:, sl].astype(F32), kg).T.astype(kt_ref.dtype)
        vo_ref[g] = v_ref[:, sl]


def _attention_prep(proj, q_norm_g, k_norm_g, tables, dm):
    t = proj.shape[0]
    tp = _largest_divisor(math.gcd(dm.l, dm.ctx), (256, 128))
    mw, kvw, hd = dm.mw, dm.kvw, dm.hd
    n_lat_tiles = dm.n_lat // tp

    def batch_of(i):
        return jnp.where(i < n_lat_tiles, (i * tp) // dm.l, (i * tp - dm.n_lat) // dm.ctx)

    def key_block(i):
        lat = (dm.ctx + (i * tp) % dm.l) // tp
        ctx = ((i * tp - dm.n_lat) % dm.ctx) // tp
        return jnp.where(i < n_lat_tiles, lat, ctx)

    def table_block(i):
        lat = ((i * tp) % dm.l) // tp
        ctx = dm.l // tp + ((i * tp - dm.n_lat) % dm.ctx) // tp
        return jnp.where(i < n_lat_tiles, lat, ctx)

    tab_spec = pl.BlockSpec((tp, hd), lambda i: (table_block(i), 0))
    qscale = hd ** -0.5 * LOG2E
    return pl.pallas_call(
        functools.partial(_prep_kernel, nq=dm.nq, nkv=dm.nkv, hd=hd, qscale=qscale),
        grid=(t // tp,),
        in_specs=[
            pl.BlockSpec((tp, mw), lambda i: (i, dm.c_aq // mw)),
            pl.BlockSpec((tp, kvw), lambda i: (i, dm.c_ak // kvw)),
            pl.BlockSpec((tp, kvw), lambda i: (i, dm.c_av // kvw)),
            pl.BlockSpec((1, hd), lambda i: (0, 0)),
            pl.BlockSpec((1, hd), lambda i: (0, 0)),
            tab_spec, tab_spec, tab_spec,
        ],
        out_specs=[
            pl.BlockSpec((tp, mw), lambda i: (i, 0)),
            pl.BlockSpec((None, dm.nkv, hd, tp), lambda i: (batch_of(i), 0, 0, key_block(i))),
            pl.BlockSpec((None, dm.nkv, tp, hd), lambda i: (batch_of(i), 0, key_block(i), 0)),
        ],
        out_shape=[
            jax.ShapeDtypeStruct((t, mw), BF16),
            jax.ShapeDtypeStruct((dm.b, dm.nkv, hd, dm.lk), BF16),
            jax.ShapeDtypeStruct((dm.b, dm.nkv, dm.lk, hd), BF16),
        ],
        compiler_params=_cparams("parallel"),
        name="attention_prep",
    )(proj, proj, proj, q_norm_g.reshape(1, hd), k_norm_g.reshape(1, hd), *tables)


def _attn_kernel(q_ref, kt_ref, v_ref, o_ref, m_ref, l_ref, acc_ref, kmax_ref, *,
                 hd, n_lat_tiles, lat_keys, ctx_keys):
    tq = q_ref.shape[0]
    i = pl.program_id(2)
    q = q_ref[...]
    qs = jnp.concatenate([q[:, h * hd:(h + 1) * hd] for h in range(GQA_GROUP)], axis=0)

    @pl.when(i == 0)
    def _():
        k = kt_ref[...].astype(F32)
        k2 = jnp.max(jnp.sum(k * k, axis=0, keepdims=True), axis=1, keepdims=True)
        kmax_ref[...] = jnp.broadcast_to(jnp.sqrt(k2), kmax_ref.shape)

    qf = qs.astype(F32)
    shift = jnp.sqrt(jnp.sum(qf * qf, axis=-1, keepdims=True)) * kmax_ref[0:1, 0:1]
    bounded = jnp.max(shift) <= MAX_SAFE_SHIFT

    def finish(l):
        o = acc_ref[...] * (1.0 / l)
        o_ref[...] = jnp.concatenate([o[h * tq:(h + 1) * tq] for h in range(GQA_GROUP)],
                                     axis=1).astype(o_ref.dtype)

    def attend_bounded(tk, nk):
        cb = _largest_divisor(tk, (256, 128))
        m_ref[...] = jnp.broadcast_to(shift, m_ref.shape)
        l_ref[...] = jnp.zeros(l_ref.shape, F32)
        acc_ref[...] = jnp.zeros(acc_ref.shape, F32)

        def body(kt, carry):
            off = kt * tk
            mm = m_ref[...]
            mm = jnp.concatenate([mm] * (cb // 128), axis=1)
            lsum = l_ref[...]
            parts = []
            for c in range(tk // cb):
                col = pl.multiple_of(off + c * cb, cb)
                p = jnp.exp2(jnp.dot(qs, kt_ref[:, pl.ds(col, cb)], preferred_element_type=F32) - mm)
                for u in range(cb // 128):
                    lsum = lsum + p[:, u * 128:(u + 1) * 128]
                parts.append(p.astype(BF16))
            l_ref[...] = lsum
            acc_ref[...] += jnp.dot(jnp.concatenate(parts, axis=1), v_ref[pl.ds(pl.multiple_of(off, tk), tk), :],
                                    preferred_element_type=F32)
            return carry

        lax.fori_loop(0, nk, body, 0, unroll=True)
        finish(jnp.sum(l_ref[...], axis=-1, keepdims=True))

    def attend_online(tk, nk):
        m_ref[...] = jnp.full(m_ref.shape, -jnp.inf, F32)
        l_ref[...] = jnp.zeros(l_ref.shape, F32)
        acc_ref[...] = jnp.zeros(acc_ref.shape, F32)

        def body(kt, carry):
            off = pl.multiple_of(kt * tk, tk)
            s = jnp.dot(qs, kt_ref[:, pl.ds(off, tk)], preferred_element_type=F32)
            m_prev = m_ref[:, 0:1]
            m_new = jnp.maximum(m_prev, jnp.max(s, axis=-1, keepdims=True))
            p = jnp.exp2(s - m_new)
            a = jnp.exp2(m_prev - m_new)
            l_ref[:, 0:1] = a * l_ref[:, 0:1] + jnp.sum(p, axis=-1, keepdims=True)
            acc_ref[...] = a * acc_ref[...] + jnp.dot(p.astype(BF16), v_ref[pl.ds(off, tk), :],
                                                      preferred_element_type=F32)
            m_ref[:, 0:1] = m_new
            return carry

        lax.fori_loop(0, nk, body, 0)
        finish(l_ref[:, 0:1])

    def attend(keys):
        @pl.when(bounded)
        def _():
            attend_bounded(*keys)

        @pl.when(jnp.logical_not(bounded))
        def _():
            attend_online(*keys)

    is_latent = i < n_lat_tiles

    @pl.when(is_latent)
    def _():
        attend(lat_keys)

    if ctx_keys is not None:
        @pl.when(jnp.logical_not(is_latent))
        def _():
            attend(ctx_keys)


def _attention(qp, kt, v, dm, *, with_ctx):
    hd, mw = dm.hd, dm.mw
    tq = _largest_divisor(math.gcd(dm.l, dm.ctx), (256, 128))
    key_tiling = lambda n: (lambda tk: (tk, n // tk))(_largest_divisor(n, (768, 512, 256, 128)))
    gw = GQA_GROUP * hd
    nl, nc = dm.l // tq, dm.ctx // tq
    rows = dm.n_tok if with_ctx else dm.n_lat

    def q_map(b, g, i):
        return jnp.where(i < nl, b * nl + i, dm.n_lat // tq + b * nc + (i - nl)), g

    return pl.pallas_call(
        functools.partial(_attn_kernel, hd=hd, n_lat_tiles=nl, lat_keys=key_tiling(dm.lk),
                          ctx_keys=key_tiling(dm.ctx) if with_ctx else None),
        grid=(dm.b, dm.nkv, nl + (nc if with_ctx else 0)),
        in_specs=[
            pl.BlockSpec((tq, gw), q_map),
            pl.BlockSpec((None, None, hd, dm.lk), lambda b, g, i: (b, g, 0, 0)),
            pl.BlockSpec((None, None, dm.lk, hd), lambda b, g, i: (b, g, 0, 0)),
        ],
        out_specs=pl.BlockSpec((tq, gw), q_map),
        out_shape=jax.ShapeDtypeStruct((rows, mw), BF16),
        scratch_shapes=[
            pltpu.VMEM((GQA_GROUP * tq, 128), F32),
            pltpu.VMEM((GQA_GROUP * tq, 128), F32),
            pltpu.VMEM((GQA_GROUP * tq, hd), F32),
            pltpu.VMEM((8, 128), F32),
        ],
        compiler_params=_cparams("parallel", "parallel", "arbitrary"),
        name="attention",
    )(qp, kt, v)


def _sgu_kernel(u_ref, v_ref, g_ref, b_ref, w_ref, bs_ref, o_ref, *, ng, gw, chunk):
    ts = u_ref.shape[0]
    u = _gelu_tanh(u_ref[...].astype(F32))
    v = _ln(_gelu_tanh(v_ref[...].astype(F32))) * g_ref[...] + b_ref[...]
    vb = v.astype(BF16)
    for c in range(ts // chunk):
        rows = slice(c * chunk, (c + 1) * chunk)
        for g in range(ng):
            cols = slice(g * gw, (g + 1) * gw)
            z = jnp.dot(w_ref[g], vb[rows, cols], preferred_element_type=F32) + bs_ref[g]
            o_ref[rows, cols] = (u[rows, cols] * z).astype(o_ref.dtype)


def _sgu(proj, ln_g, ln_b, w_s, b_s, dm, rows):
    mw, chunk, ng = dm.mw, dm.sgu_chunk, dm.sgu_groups
    ts = _largest_divisor(math.gcd(dm.l, dm.n_ctx), (256, 128))
    return pl.pallas_call(
        functools.partial(_sgu_kernel, ng=ng, gw=mw // ng, chunk=chunk),
        grid=(rows // ts,),
        in_specs=[
            pl.BlockSpec((ts, mw), lambda i: (i, dm.c_u // mw)),
            pl.BlockSpec((ts, mw), lambda i: (i, dm.c_v // mw)),
            pl.BlockSpec((1, mw), lambda i: (0, 0)),
            pl.BlockSpec((1, mw), lambda i: (0, 0)),
            pl.BlockSpec((ng, chunk, chunk), lambda i: (0, 0, 0)),
            pl.BlockSpec((ng, chunk, 1), lambda i: (0, 0, 0)),
        ],
        out_specs=pl.BlockSpec((ts, mw), lambda i: (i, 0)),
        out_shape=jax.ShapeDtypeStruct((rows, mw), BF16),
        compiler_params=_cparams("parallel"),
        name="sgu",
    )(proj, proj, ln_g.reshape(1, mw), ln_b.reshape(1, mw), w_s.astype(BF16), b_s.reshape(ng, chunk, 1))


def _ret_tables(decay_logit, dm, backward):
    c, hd, nh = dm.ret_chunk, dm.hd, dm.ret_heads
    lg = jax.nn.log_sigmoid(decay_logit.astype(F32))
    pos = jnp.arange(c, dtype=F32)
    diff = pos[:, None] - pos[None, :]
    if backward:
        diff = -diff
        xi_e = c - pos
        zeta_e = pos
    else:
        xi_e = pos + 1.0
        zeta_e = c - 1.0 - pos
    dmat = jnp.where(diff >= 0, jnp.exp(jnp.maximum(diff, 0.0)[None] * lg[:, None, None]), 0.0)
    dmat = dmat * hd ** -0.5
    xi = jnp.exp(xi_e[None, :] * lg[:, None])[..., None]
    zeta = (jnp.exp(zeta_e[None, :] * lg[:, None]) * hd ** -0.5)[..., None]
    dec = jnp.broadcast_to(jnp.exp(c * lg)[:, None, None], (nh, 1, hd))
    return dmat.astype(F32), xi.astype(F32), zeta.astype(F32), dec.astype(F32)


def _ret_kernel(*refs, nh, hd, final):
    if final:
        q_ref, k_ref, v_ref, dm_ref, xi_ref, zeta_ref, dec_ref, of_ref, g_ref, o_ref, st_ref = refs
    else:
        q_ref, k_ref, v_ref, dm_ref, xi_ref, zeta_ref, dec_ref, o_ref, st_ref = refs

    @pl.when(pl.program_id(1) == 0)
    def _():
        st_ref[...] = jnp.zeros(st_ref.shape, F32)

    for h in range(nh):
        sl = slice(h * hd, (h + 1) * hd)
        qh, kh, vh = q_ref[:, sl], k_ref[:, sl], v_ref[:, sl]
        s = lax.dot_general(qh, kh, (((1,), (1,)), ((), ())), preferred_element_type=F32)
        inner = (s * dm_ref[h]).astype(BF16)
        st = st_ref[h]
        o = jnp.dot(inner, vh, preferred_element_type=F32)
        o = o + jnp.dot(qh, st.astype(BF16), preferred_element_type=F32) * xi_ref[h]
        kz = (kh.astype(F32) * zeta_ref[h]).astype(BF16)
        st_ref[h] = st * dec_ref[h] + lax.dot_general(kz, vh, (((0,), (0,)), ((), ())),
                                                      preferred_element_type=F32)
        if final:
            gn = _ln(of_ref[:, sl] + o)
            o_ref[:, sl] = (_silu(g_ref[:, sl].astype(F32)) * gn).astype(o_ref.dtype)
        else:
            o_ref[:, sl] = o


def _retention_pass(proj, tables, dm, *, backward, o_fwd=None):
    c, mw, nh, hd = dm.ret_chunk, dm.mw, dm.ret_heads, dm.hd
    n_cc, n_lc = dm.ctx // c, dm.l // c
    steps = n_cc + n_lc

    def row_block(b, t):
        if backward:
            ctx = dm.n_lat // c + b * n_cc + (n_cc - 1 - t)
            lat = b * n_lc + (n_lc - 1 - (t - n_cc))
        else:
            ctx = dm.n_lat // c + b * n_cc + t
            lat = b * n_lc + (t - n_cc)
        return jnp.where(t < n_cc, ctx, lat)

    col = lambda off: (lambda b, t: (row_block(b, t), off // mw))
    const3 = lambda b, t: (0, 0, 0)
    in_specs = [
        pl.BlockSpec((c, mw), col(dm.c_rq)),
        pl.BlockSpec((c, mw), col(dm.c_rk)),
        pl.BlockSpec((c, mw), col(dm.c_rv)),
        pl.BlockSpec((nh, c, c), const3),
        pl.BlockSpec((nh, c, 1), const3),
        pl.BlockSpec((nh, c, 1), const3),
        pl.BlockSpec((nh, 1, hd), const3),
    ]
    args = [proj, proj, proj, *tables]
    final = o_fwd is not None
    if final:
        in_specs += [pl.BlockSpec((c, mw), col(0)), pl.BlockSpec((c, mw), col(dm.c_rg))]
        args += [o_fwd, proj]
    return pl.pallas_call(
        functools.partial(_ret_kernel, nh=nh, hd=hd, final=final),
        grid=(dm.b, steps),
        in_specs=in_specs,
        out_specs=pl.BlockSpec((c, mw), col(0)),
        out_shape=jax.ShapeDtypeStruct((dm.n_tok, mw), BF16 if final else F32),
        scratch_shapes=[pltpu.VMEM((nh, hd, hd), F32)],
        compiler_params=_cparams("parallel", "arbitrary"),
        name="retention_bwd" if final else "retention_fwd",
    )(*args)


def _merge_kernel(att_ref, sgu_ref, ret_ref, ga_ref, gs_ref, gr_ref, wa_ref, ws_ref, wr_ref, wo_ref,
                  x_ref, g1_ref, lng_ref, lnb_ref, o_ref, *, alpha):
    def branch(a_ref, w_ref, gate_ref):
        return jax.nn.sigmoid(gate_ref[...].astype(F32)) * jnp.dot(a_ref[...], w_ref[...],
                                                                    preferred_element_type=F32)

    m = branch(att_ref, wa_ref, ga_ref) + branch(sgu_ref, ws_ref, gs_ref) + branch(ret_ref, wr_ref, gr_ref)
    mix = jnp.dot(m.astype(BF16), wo_ref[...], preferred_element_type=F32)
    y = alpha * x_ref[...] + g1_ref[...] * mix
    o_ref[...] = _ln(y) * lng_ref[...] + lnb_ref[...]


def _merge(att, sgu, ret, proj, w_att, w_sgu, w_ret, w_o, xs, mods, ln_g, ln_b, dm, layer, rows):
    d, mw = dm.d, dm.mw
    tm = _largest_divisor(math.gcd(dm.l, dm.n_ctx), (256, 128))
    row = dm.mod_row
    tok = lambda w: pl.BlockSpec((tm, w), lambda i: (i, 0))
    gate = lambda k: pl.BlockSpec((tm, d), lambda i: (i, dm.c_bg // d + k))
    vec = pl.BlockSpec((1, d), lambda i: (0, 0))
    return pl.pallas_call(
        functools.partial(_merge_kernel, alpha=dm.alpha),
        grid=(rows // tm,),
        in_specs=[
            tok(mw), tok(mw), tok(mw), gate(0), gate(1), gate(2),
            _resident((mw, d), lambda i: (0, 0)), _resident((mw, d), lambda i: (0, 0)),
            _resident((mw, d), lambda i: (0, 0)), _resident((d, d), lambda i: (0, 0)),
            tok(d),
            pl.BlockSpec((None, 1, d), lambda i: (row(layer, 2)(i, tm), 0, 0)),
            vec, vec,
        ],
        out_specs=tok(d),
        out_shape=jax.ShapeDtypeStruct((rows, d), F32),
        compiler_params=_cparams("parallel"),
        name="merge",
    )(att, sgu, ret, proj, proj, proj, w_att, w_sgu, w_ret, w_o, xs, mods,
      ln_g.reshape(1, d), ln_b.reshape(1, d))


def _ffn_kernel(x_ref, sh_ref, sc_ref, w1_ref, w3_ref, w2_ref, g2_ref, lng_ref, lnb_ref, o_ref,
                h_ref, acc_ref, *, alpha):
    j = pl.program_id(1)

    @pl.when(j == 0)
    def _():
        h = _ln(x_ref[...]) * (1.0 + sc_ref[...]) + sh_ref[...]
        h_ref[...] = h.astype(BF16)
        acc_ref[...] = jnp.zeros(acc_ref.shape, F32)

    h = h_ref[...]
    a = jnp.dot(h, w1_ref[...], preferred_element_type=F32)
    b = jnp.dot(h, w3_ref[...], preferred_element_type=F32)
    acc_ref[...] += jnp.dot((_silu(a) * b).astype(BF16), w2_ref[...], preferred_element_type=F32)

    @pl.when(j == pl.num_programs(1) - 1)
    def _():
        y = alpha * x_ref[...] + g2_ref[...] * acc_ref[...]
        o_ref[...] = _ln(y) * lng_ref[...] + lnb_ref[...]


def _ffn(xs, mods, w1, w3, w2, ln_g, ln_b, dm, layer):
    t, d = xs.shape
    f = w1.shape[1]
    tm = _largest_divisor(math.gcd(dm.l, dm.n_ctx), (512, 256, 128))
    tf = _largest_divisor(f, (512, 256, 128))
    row = dm.mod_row
    mod = lambda k: pl.BlockSpec((None, 1, d), lambda i, j: (row(layer, k)(i, tm), 0, 0))
    vec = pl.BlockSpec((1, d), lambda i, j: (0, 0))
    return pl.pallas_call(
        functools.partial(_ffn_kernel, alpha=dm.alpha),
        grid=(t // tm, f // tf),
        in_specs=[
            pl.BlockSpec((tm, d), lambda i, j: (i, 0)),
            mod(3), mod(4),
            pl.BlockSpec((d, tf), lambda i, j: (0, j)),
            pl.BlockSpec((d, tf), lambda i, j: (0, j)),
            pl.BlockSpec((tf, d), lambda i, j: (j, 0)),
            mod(5), vec, vec,
        ],
        out_specs=pl.BlockSpec((tm, d), lambda i, j: (i, 0)),
        out_shape=jax.ShapeDtypeStruct((t, d), F32),
        scratch_shapes=[pltpu.VMEM((tm, d), BF16), pltpu.VMEM((tm, d), F32)],
        compiler_params=_cparams("parallel", "arbitrary"),
        name="ffn_dense",
    )(xs, mods, mods, w1, w3, w2, mods, ln_g.reshape(1, d), ln_b.reshape(1, d))


def _router_kernel(x_ref, sh_ref, sc_ref, whi_ref, wlo_ref, tril_ref, h_ref, gates_ref, code_ref, rank_ref,
                   count_ref):
    h = _ln(x_ref[...]) * (1.0 + sc_ref[...]) + sh_ref[...]
    _store_token_major(h_ref, h)
    h_hi = h.astype(BF16)
    h_lo = (h - h_hi.astype(F32)).astype(BF16)
    w_hi = whi_ref[...]
    logits = (jnp.dot(h_hi, w_hi, preferred_element_type=F32)
              + jnp.dot(h_lo, w_hi, preferred_element_type=F32)
              + jnp.dot(h_hi, wlo_ref[...], preferred_element_type=F32))
    n_exp = float(logits.shape[-1])
    ids = lax.broadcasted_iota(jnp.int32, logits.shape, 1).astype(F32)
    m1 = jnp.max(logits, axis=-1, keepdims=True)
    i1 = jnp.min(jnp.where(logits == m1, ids, n_exp), axis=-1, keepdims=True)
    rest = jnp.where(ids == i1, -jnp.inf, logits)
    m2 = jnp.max(rest, axis=-1, keepdims=True)
    i2 = jnp.min(jnp.where(rest == m2, ids, n_exp), axis=-1, keepdims=True)
    e = jnp.exp(m2 - m1)
    w1 = 1.0 / (1.0 + e)
    w2 = e * w1
    gates_ref[...] = jnp.where(ids == i1, w1, jnp.where(ids == i2, w2, 0.0))
    code_ref[...] = jnp.where(ids == i1, 1, jnp.where(ids == i2, 2, 0)).astype(jnp.int32)

    @pl.when(pl.program_id(0) == 0)
    def _():
        count_ref[...] = jnp.zeros(count_ref.shape, F32)

    chosen = jnp.where(jnp.logical_or(ids == i1, ids == i2), 1.0, 0.0)
    before = jnp.dot(tril_ref[...], chosen.astype(BF16), preferred_element_type=F32) + count_ref[...]
    rank_ref[...] = before.astype(jnp.int32)
    count_ref[...] = count_ref[...] + jnp.sum(chosen, axis=0, keepdims=True)


def _router(xs, mods, router_w, dm, layer, rows):
    d, n_exp = router_w.shape
    tm = _largest_divisor(dm.l, (512, 256, 128))
    row = dm.mod_row
    mod = lambda k: pl.BlockSpec((None, 1, d), lambda i: (row(layer, k)(i, tm), 0, 0))
    w_hi = router_w.astype(BF16)
    w_lo = (router_w - w_hi.astype(F32)).astype(BF16)
    return pl.pallas_call(
        _router_kernel,
        grid=(rows // tm,),
        in_specs=[
            pl.BlockSpec((tm, d), lambda i: (i, 0)), mod(3), mod(4),
            pl.BlockSpec((d, n_exp), lambda i: (0, 0)),
            pl.BlockSpec((d, n_exp), lambda i: (0, 0)),
            pl.BlockSpec((tm, tm), lambda i: (0, 0)),
        ],
        out_specs=[
            pl.BlockSpec((tm * (d // LANES), LANES), lambda i: (i, 0)),
            pl.BlockSpec((tm, n_exp), lambda i: (i, 0)),
            pl.BlockSpec((tm, n_exp), lambda i: (i, 0)),
            pl.BlockSpec((tm, n_exp), lambda i: (i, 0)),
        ],
        out_shape=[
            jax.ShapeDtypeStruct((rows * (d // LANES), LANES), F32),
            jax.ShapeDtypeStruct((rows, n_exp), F32),
            jax.ShapeDtypeStruct((rows, n_exp), jnp.int32),
            jax.ShapeDtypeStruct((rows, n_exp), jnp.int32),
        ],
        scratch_shapes=[pltpu.VMEM((1, n_exp), F32)],
        compiler_params=_cparams("arbitrary"),
        name="router",
    )(xs, mods, mods, w_hi, w_lo, jnp.tril(jnp.ones((tm, tm), BF16), -1))


def _token_rows(ref, t, g):
    return ref.at[pl.ds(pl.multiple_of(t * g, g), g), :]


def _row_gather(index_of, lo, hi, hbm_ref, buf_ref, sem, g, *, wait):
    def body(q, carry):
        for u in range(ROW_DMA_UNROLL):
            r = lo + q * ROW_DMA_UNROLL + u
            cp = pltpu.make_async_copy(_token_rows(hbm_ref, index_of(r), g), _token_rows(buf_ref, r, g), sem)
            if wait:
                cp.wait()
            else:
                cp.start(priority=0)
        return carry

    lax.fori_loop(0, (hi - lo) // ROW_DMA_UNROLL, body, 0)


def _row_scatter(index_of, lo, hi, buf_ref, hbm_ref, sem, g, *, wait):
    def body(r, carry):
        cp = pltpu.make_async_copy(_token_rows(buf_ref, r, g), _token_rows(hbm_ref, index_of(r), g), sem)
        if wait:
            cp.wait()
        else:
            cp.start(priority=1)
        return carry

    lax.fori_loop(lo, hi, body, 0)


def _expert_kernel(te_ref, tv_ref, tok_cur, tok_nxt, dst_prv, h_hbm, w1_ref, w3_ref, w2_ref, y_hbm,
                   in_buf, out_buf, in_sem, out_sem, h_ref, acc_ref, *, nj):
    i, j = pl.program_id(0), pl.program_id(1)
    nt = pl.num_programs(0)
    tm = h_ref.shape[0]
    g = h_ref.shape[1] // LANES
    slot = i % 2
    chunk = -(-tm // (nj * ROW_DMA_UNROLL)) * ROW_DMA_UNROLL
    lo, hi = jnp.minimum(j * chunk, tm), jnp.minimum((j + 1) * chunk, tm)
    valid = tv_ref[i] > 0
    prev_rows = jnp.where(i > 0, tv_ref[jnp.maximum(i - 1, 0)], 0)
    prev_valid = prev_rows > 0
    next_valid = jnp.logical_and(i + 1 < nt, tv_ref[jnp.minimum(i + 1, nt - 1)] > 0)
    gather = functools.partial(_row_gather, hbm_ref=h_hbm, g=g)
    scatter = functools.partial(_row_scatter, lambda r: dst_prv[0, 0, r], buf_ref=out_buf.at[1 - slot],
                                hbm_ref=y_hbm, sem=out_sem.at[1 - slot], g=g)

    @pl.when(jnp.logical_and(valid, j == 0))
    def _():
        @pl.when(i == 0)
        def _():
            gather(lambda r: tok_cur[0, 0, r], 0, tm, buf_ref=in_buf.at[0], sem=in_sem.at[0], wait=False)

        gather(lambda r: tok_cur[0, 0, r], 0, tm, buf_ref=in_buf.at[slot], sem=in_sem.at[slot], wait=True)
        h_ref[...] = _load_token_major(in_buf.at[slot], tm, g).astype(BF16)
        acc_ref[...] = jnp.zeros(acc_ref.shape, F32)

    @pl.when(next_valid)
    def _():
        gather(lambda r: tok_nxt[0, 0, r], lo, hi, buf_ref=in_buf.at[1 - slot], sem=in_sem.at[1 - slot], wait=False)

    @pl.when(prev_valid)
    def _():
        scatter(jnp.minimum(lo, prev_rows), jnp.minimum(hi, prev_rows), wait=False)

    @pl.when(valid)
    def _():
        h = h_ref[...]
        a = jnp.dot(h, w1_ref[...], preferred_element_type=F32)
        b = jnp.dot(h, w3_ref[...], preferred_element_type=F32)
        acc_ref[...] += jnp.dot((_silu(a) * b).astype(BF16), w2_ref[...], preferred_element_type=F32)

        @pl.when(j == nj - 1)
        def _():
            _store_token_major(out_buf.at[slot], acc_ref[...])

    @pl.when(jnp.logical_and(prev_valid, j == nj - 1))
    def _():
        scatter(0, prev_rows, wait=True)


def _expert_mlp(h_tm, slot_token, slot_dest, tile_expert, tile_valid, w1, w3, w2, tm, d, n_out):
    n_slots = slot_token.shape[0]
    f = w1.shape[2]
    g = d // LANES
    tf = _largest_divisor(f, (1024, 512, 256, 128))
    nj, nt = f // tf, n_slots // tm
    jj = lambda i, j, tv: jnp.where(tv[i] > 0, j, nj - 1)
    tok = slot_token.reshape(nt, 1, tm)
    dst = slot_dest.reshape(nt, 1, tm)
    smem = lambda f_i: pl.BlockSpec((1, 1, tm), lambda i, j, te, tv: (f_i(i), 0, 0), memory_space=pltpu.SMEM)
    grid_spec = pltpu.PrefetchScalarGridSpec(
        num_scalar_prefetch=2,
        grid=(nt, nj),
        in_specs=[
            smem(lambda i: i),
            smem(lambda i: jnp.minimum(i + 1, nt - 1)),
            smem(lambda i: jnp.maximum(i - 1, 0)),
            pl.BlockSpec(memory_space=pl.ANY),
            pl.BlockSpec((None, d, tf), lambda i, j, te, tv: (te[i], 0, jj(i, j, tv))),
            pl.BlockSpec((None, d, tf), lambda i, j, te, tv: (te[i], 0, jj(i, j, tv))),
            pl.BlockSpec((None, tf, d), lambda i, j, te, tv: (te[i], jj(i, j, tv), 0)),
        ],
        out_specs=pl.BlockSpec(memory_space=pl.ANY),
        scratch_shapes=[
            pltpu.VMEM((2, tm * g, LANES), F32),
            pltpu.VMEM((2, tm * g, LANES), F32),
            pltpu.SemaphoreType.DMA((2,)),
            pltpu.SemaphoreType.DMA((2,)),
            pltpu.VMEM((tm, d), BF16),
            pltpu.VMEM((tm, d), F32),
        ],
    )
    return pl.pallas_call(
        functools.partial(_expert_kernel, nj=nj),
        grid_spec=grid_spec,
        out_shape=jax.ShapeDtypeStruct((n_out * g, LANES), F32),
        compiler_params=_cparams("arbitrary", "arbitrary"),
        name="expert_mlp",
    )(tile_expert, tile_valid, tok, tok, dst, h_tm, w1, w3, w2)


def _moe_out_kernel(y1_ref, y2_ref, gates_ref, code_ref, x_ref, g2_ref, lng_ref, lnb_ref, o_ref, *, alpha):
    tm, d = x_ref.shape
    g = d // LANES
    gates, code = gates_ref[...], code_ref[...]
    f = jnp.zeros((tm, d), F32)
    for k, y_ref in enumerate((y1_ref, y2_ref)):
        w = jnp.sum(jnp.where(code == k + 1, gates, 0.0), axis=-1, keepdims=True)
        f = f + w * _load_token_major(y_ref, tm, g)
    y = alpha * x_ref[...] + g2_ref[...] * f
    o_ref[...] = _ln(y) * lng_ref[...] + lnb_ref[...]


def _moe_out(y_tm, gates, code, xs, mods, ln_g, ln_b, dm, layer, rows):
    d, n_exp = dm.d, gates.shape[1]
    g = d // LANES
    tm = _largest_divisor(dm.l, (256, 128))
    nt = rows // tm
    row = dm.mod_row
    vec = pl.BlockSpec((1, d), lambda i: (0, 0))
    return pl.pallas_call(
        functools.partial(_moe_out_kernel, alpha=dm.alpha),
        grid=(nt,),
        in_specs=[
            pl.BlockSpec((tm * g, LANES), lambda i: (i, 0)),
            pl.BlockSpec((tm * g, LANES), lambda i: (nt + i, 0)),
            pl.BlockSpec((tm, n_exp), lambda i: (i, 0)),
            pl.BlockSpec((tm, n_exp), lambda i: (i, 0)),
            pl.BlockSpec((tm, d), lambda i: (i, 0)),
            pl.BlockSpec((None, 1, d), lambda i: (row(layer, 5)(i, tm), 0, 0)),
            vec, vec,
        ],
        out_specs=pl.BlockSpec((tm, d), lambda i: (i, 0)),
        out_shape=jax.ShapeDtypeStruct((rows, d), F32),
        compiler_params=_cparams("parallel"),
        name="moe_out",
    )(y_tm, y_tm, gates, code, xs, mods, ln_g.reshape(1, d), ln_b.reshape(1, d))


def _moe(xs, mods, router_w, w1, w3, w2, ln_g, ln_b, dm, layer, rows):
    n_exp = router_w.shape[1]
    tm = 512 if rows * TOP_K >= 512 * n_exp * 4 else 128
    h, gates, code, rank = _router(xs, mods, router_w, dm, layer, rows)

    counts = jnp.sum((code > 0).astype(jnp.int32), axis=0)
    padded = ((counts + tm - 1) // tm) * tm
    ends = jnp.cumsum(padded)
    starts = ends - padded
    slot_of = starts[None, :] + rank
    slot = jnp.concatenate([jnp.sum(jnp.where(code == k + 1, slot_of, 0), axis=1) for k in range(TOP_K)])
    n_slots = TOP_K * rows + (n_exp + 1) * tm
    assignment = jnp.arange(TOP_K * rows, dtype=jnp.int32)
    slot_dest = jnp.full((n_slots,), -1, jnp.int32).at[slot].set(assignment)
    slot_token = jnp.maximum(slot_dest, 0) % rows
    tile_start = jnp.arange(n_slots // tm, dtype=jnp.int32) * tm
    last_used = jnp.minimum(tile_start, ends[-1] - 1)
    tile_expert = jnp.minimum(jnp.sum((ends[None, :] <= last_used[:, None]).astype(jnp.int32), axis=1),
                              n_exp - 1).astype(jnp.int32)
    tile_rows = jnp.where(tile_start < ends[-1],
                          jnp.clip((starts + counts)[tile_expert] - tile_start, 0, tm), 0).astype(jnp.int32)

    y = _expert_mlp(h, slot_token, slot_dest, tile_expert, tile_rows, w1, w3, w2, tm, dm.d, TOP_K * rows)
    return _moe_out(y, gates, code, xs, mods, ln_g, ln_b, dm, layer, rows)


def kernel(x, c, ctx, c_ctx, ada_w, ada_b, w_in, q_norm_g, k_norm_g, sgu_ln_g, sgu_ln_b, sgu_w, sgu_b,
           ret_decay_fwd, ret_decay_bwd, w_br_att, w_br_sgu, w_br_ret, w_out, ln1_g, ln1_b, ln2_g, ln2_b,
           ffn_w1, ffn_w3, ffn_w2, router_w, moe_w1, moe_w3, moe_w2):
    dm = _Dims(x, ctx, ada_w, q_norm_g, sgu_w, ret_decay_fwd, ffn_w1, moe_w1)
    assert dm.b + 1 <= MOD_ROWS and dm.ret_heads == dm.nq
    d = dm.d

    cond = jnp.concatenate([c, c_ctx[None], jnp.zeros((MOD_ROWS - dm.b - 1, d), F32)], axis=0)
    mods = _ada_mods(cond, ada_w, ada_b)
    rope = _rope_tables(dm)

    xs = jnp.concatenate([x.reshape(dm.n_lat, d), ctx.reshape(dm.n_ctx, d)], axis=0)
    for i in range(dm.depth):
        last = i == dm.depth - 1
        rows = dm.n_lat if last else dm.n_tok

        proj = _in_projection(xs, mods, _reorder_w_in(w_in[i], dm), dm, i)

        qp, kt, v_all = _attention_prep(proj, q_norm_g[i], k_norm_g[i], rope, dm)
        att = _attention(qp, kt, v_all, dm, with_ctx=not last)

        sgu = _sgu(proj, sgu_ln_g[i], sgu_ln_b[i], sgu_w[i], sgu_b[i], dm, rows)

        o_fwd = _retention_pass(proj, _ret_tables(ret_decay_fwd[i], dm, False), dm, backward=False)
        ret = _retention_pass(proj, _ret_tables(ret_decay_bwd[i], dm, True), dm, backward=True, o_fwd=o_fwd)

        xs1 = _merge(att, sgu, ret, proj, w_br_att[i].astype(BF16), w_br_sgu[i].astype(BF16),
                     w_br_ret[i].astype(BF16), w_out[i].astype(BF16), xs, mods, ln1_g[i], ln1_b[i], dm, i, rows)

        j = i // 2
        if i % 2 == 0:
            assert rows == xs1.shape[0]
            xs = _ffn(xs1, mods, ffn_w1[j].astype(BF16), ffn_w3[j].astype(BF16), ffn_w2[j].astype(BF16),
                      ln2_g[i], ln2_b[i], dm, i)
        else:
            xs = _moe(xs1, mods, router_w[j], moe_w1[j].astype(BF16), moe_w3[j].astype(BF16),
                      moe_w2[j].astype(BF16), ln2_g[i], ln2_b[i], dm, i, rows)
    return xs[:dm.n_lat].reshape(dm.b, dm.l, d)
```

```python
import functools
import math

import jax
import jax.numpy as jnp
from jax import lax
from jax.experimental import pallas as pl
from jax.experimental.pallas import tpu as pltpu

F32 = jnp.float32
BF16 = jnp.bfloat16
EPS = 1e-6
GRID_W = 64
ROPE_THETA = 10000.0
GQA_GROUP = 4
TOP_K = 2
N_MODS = 6
MOD_ROWS = 8
ROW_DMA_UNROLL = 2
V7X_VMEM_BYTES = 64 * 1024 * 1024
VMEM_LIMIT = V7X_VMEM_BYTES - 8 * 1024 * 1024
LOG2E = 1.4426950408889634
MAX_SAFE_SHIFT = 60.0


def _cparams(*sem):
    return pltpu.CompilerParams(dimension_semantics=sem, vmem_limit_bytes=VMEM_LIMIT)


def _resident(shape, index_map):
    return pl.BlockSpec(shape, index_map, pipeline_mode=pl.Buffered(1))


def _largest_divisor(n, candidates):
    for c in candidates:
        if n % c == 0:
            return c
    raise ValueError(f"no tile in {candidates} divides {n}")


def _ln(x):
    mu = jnp.mean(x, axis=-1, keepdims=True)
    xc = x - mu
    var = jnp.mean(xc * xc, axis=-1, keepdims=True)
    return xc * lax.rsqrt(var + EPS)


def _silu(x):
    return x * jax.nn.sigmoid(x)


def _gelu_tanh(x):
    return 0.5 * x * (1.0 + jnp.tanh(math.sqrt(2.0 / math.pi) * (x + 0.044715 * (x * x * x))))


def _ada_kernel(c_ref, w_ref, b_ref, o_ref):
    s = _silu(c_ref[...]).astype(BF16)
    o_ref[...] = jnp.dot(s, w_ref[...].astype(BF16), preferred_element_type=F32) + b_ref[...]


def _ada_mods(cond, ada_w, ada_b):
    depth, d, nd = ada_w.shape
    tn = _largest_divisor(nd, (1024, 512, 256, 128))
    out = pl.pallas_call(
        _ada_kernel,
        grid=(depth, nd // tn),
        in_specs=[
            pl.BlockSpec((MOD_ROWS, d), lambda l, j: (0, 0)),
            pl.BlockSpec((None, d, tn), lambda l, j: (l, 0, j)),
            pl.BlockSpec((None, 1, tn), lambda l, j: (l, 0, j)),
        ],
        out_specs=pl.BlockSpec((None, MOD_ROWS, tn), lambda l, j: (l, 0, j)),
        out_shape=jax.ShapeDtypeStruct((depth, MOD_ROWS, nd), F32),
        compiler_params=_cparams("parallel", "parallel"),
        name="ada_mods",
    )(cond, ada_w, ada_b.reshape(depth, 1, nd))
    return out.reshape(depth * MOD_ROWS * N_MODS, 1, d)


class _Dims:
    def __init__(self, x, ctx, ada_w, q_norm_g, sgu_w, ret_decay_fwd, ffn_w1, moe_w1):
        self.b, self.l, self.d = x.shape
        self.ctx = ctx.shape[1]
        self.depth = ada_w.shape[0]
        self.hd = q_norm_g.shape[1]
        self.mw = self.d // 2
        self.nq = self.mw // self.hd
        self.nkv = self.nq // GQA_GROUP
        self.kvw = self.nkv * self.hd
        self.sgu_groups, self.sgu_chunk = sgu_w.shape[1], sgu_w.shape[2]
        self.ret_heads = ret_decay_fwd.shape[1]
        self.ret_chunk = _largest_divisor(math.gcd(self.l, self.ctx), (256, 128))
        self.n_lat = self.b * self.l
        self.n_ctx = self.b * self.ctx
        self.n_tok = self.n_lat + self.n_ctx
        self.lk = self.ctx + self.l
        self.n_exp = moe_w1.shape[1]
        self.alpha = (2 * self.depth) ** 0.25
        mw = self.mw
        self.c_u, self.c_v = 0, mw
        self.c_rq, self.c_rk, self.c_rv, self.c_rg = 2 * mw, 3 * mw, 4 * mw, 5 * mw
        self.c_bg = 6 * mw
        self.c_aq = 6 * mw + 3 * self.d
        self.c_ak = self.c_aq + mw
        self.c_av = self.c_ak + self.kvw
        self.n_cols = self.c_av + self.kvw

    def mod_row(self, layer, j):
        def f(i, tm):
            bidx = jnp.minimum((i * tm) // self.l, self.b)
            return (layer * MOD_ROWS + bidx) * N_MODS + j
        return f


def _reorder_w_in(w, dm):
    mw, kvw = dm.mw, dm.kvw
    aq = w[:, :mw]
    ak = w[:, mw:mw + kvw]
    av = w[:, mw + kvw:mw + 2 * kvw]
    rest = w[:, mw + 2 * kvw:]
    return jnp.concatenate([rest, aq, ak, av], axis=1).astype(BF16)


def _proj_kernel(x_ref, sh_ref, sc_ref, w_ref, o_ref, h_ref):
    @pl.when(pl.program_id(1) == 0)
    def _():
        h = _ln(x_ref[...]) * (1.0 + sc_ref[...]) + sh_ref[...]
        h_ref[...] = h.astype(BF16)

    o_ref[...] = jnp.dot(h_ref[...], w_ref[...], preferred_element_type=F32).astype(o_ref.dtype)


def _in_projection(xs, mods, w, dm, layer):
    t, d = xs.shape
    nc = w.shape[1]
    tm = _largest_divisor(math.gcd(dm.l, dm.n_ctx), (1024, 512, 256, 128))
    tn = _largest_divisor(nc, (3 * dm.mw // 2, dm.mw // 2))
    row = dm.mod_row
    return pl.pallas_call(
        _proj_kernel,
        grid=(t // tm, nc // tn),
        in_specs=[
            pl.BlockSpec((tm, d), lambda i, j: (i, 0)),
            pl.BlockSpec((None, 1, d), lambda i, j: (row(layer, 0)(i, tm), 0, 0)),
            pl.BlockSpec((None, 1, d), lambda i, j: (row(layer, 1)(i, tm), 0, 0)),
            pl.BlockSpec((d, tn), lambda i, j: (0, j)),
        ],
        out_specs=pl.BlockSpec((tm, tn), lambda i, j: (i, j)),
        out_shape=jax.ShapeDtypeStruct((t, nc), BF16),
        scratch_shapes=[pltpu.VMEM((tm, d), BF16)],
        compiler_params=_cparams("parallel", "arbitrary"),
        name="in_projection",
    )(xs, mods, mods, w)


def _rope_tables(dm):
    n_freq = dm.hd // 4
    t = jnp.arange(dm.l, dtype=F32)
    row = jnp.floor(t / GRID_W)
    col = t - row * GRID_W
    inv_freq = ROPE_THETA ** (-jnp.arange(n_freq, dtype=F32) / n_freq)
    ar, ac = row[:, None] * inv_freq, col[:, None] * inv_freq
    z = jnp.zeros_like(ar)
    cos = jnp.concatenate([jnp.cos(ar), jnp.cos(ar), jnp.cos(ac), jnp.cos(ac)], axis=1)
    sin_lo = jnp.concatenate([z, jnp.sin(ar), z, jnp.sin(ac)], axis=1)
    sin_hi = jnp.concatenate([-jnp.sin(ar), z, -jnp.sin(ac), z], axis=1)
    pad = lambda a, v: jnp.concatenate([a, jnp.full((dm.ctx, dm.hd), v, F32)], axis=0)
    return pad(cos, 1.0), pad(sin_lo, 0.0), pad(sin_hi, 0.0)


def _prep_kernel(q_ref, k_ref, v_ref, qg_ref, kg_ref, cos_ref, slo_ref, shi_ref, qo_ref, kt_ref, vo_ref,
                 *, nq, nkv, hd, qscale):
    cos, slo, shi = cos_ref[...], slo_ref[...], shi_ref[...]
    quarter = hd // 4

    def norm_rope(xh, g):
        y = xh * lax.rsqrt(jnp.mean(xh * xh, axis=-1, keepdims=True) + EPS) * g
        return y * cos + pltpu.roll(y, quarter, 1) * slo + pltpu.roll(y, hd - quarter, 1) * shi

    qg = qg_ref[...] * qscale
    kg = kg_ref[...]
    for h in range(nq):
        sl = slice(h * hd, (h + 1) * hd)
        qo_ref[:, sl] = norm_rope(q_ref[:, sl].astype(F32), qg).astype(qo_ref.dtype)
    for g in range(nkv):
        sl = slice(g * hd, (g + 1) * hd)
        kt_ref[g] = norm_rope(k_ref[:, sl].astype(F32), kg).T.astype(kt_ref.dtype)
        vo_ref[g] = v_ref[:, sl]


def _attention_prep(proj, q_norm_g, k_norm_g, tables, dm):
    t = proj.shape[0]
    tp = _largest_divisor(math.gcd(dm.l, dm.ctx), (256, 128))
    mw, kvw, hd = dm.mw, dm.kvw, dm.hd
    n_lat_tiles = dm.n_lat // tp

    def batch_of(i):
        return jnp.where(i < n_lat_tiles, (i * tp) // dm.l, (i * tp - dm.n_lat) // dm.ctx)

    def key_block(i):
        lat = (dm.ctx + (i * tp) % dm.l) // tp
        ctx = ((i * tp - dm.n_lat) % dm.ctx) // tp
        return jnp.where(i < n_lat_tiles, lat, ctx)

    def table_block(i):
        lat = ((i * tp) % dm.l) // tp
        ctx = dm.l // tp + ((i * tp - dm.n_lat) % dm.ctx) // tp
        return jnp.where(i < n_lat_tiles, lat, ctx)

    tab_spec = pl.BlockSpec((tp, hd), lambda i: (table_block(i), 0))
    qscale = hd ** -0.5 * LOG2E
    return pl.pallas_call(
        functools.partial(_prep_kernel, nq=dm.nq, nkv=dm.nkv, hd=hd, qscale=qscale),
        grid=(t // tp,),
        in_specs=[
            pl.BlockSpec((tp, mw), lambda i: (i, dm.c_aq // mw)),
            pl.BlockSpec((tp, kvw), lambda i: (i, dm.c_ak // kvw)),
            pl.BlockSpec((tp, kvw), lambda i: (i, dm.c_av // kvw)),
            pl.BlockSpec((1, hd), lambda i: (0, 0)),
            pl.BlockSpec((1, hd), lambda i: (0, 0)),
            tab_spec, tab_spec, tab_spec,
        ],
        out_specs=[
            pl.BlockSpec((tp, mw), lambda i: (i, 0)),
            pl.BlockSpec((None, dm.nkv, hd, tp), lambda i: (batch_of(i), 0, 0, key_block(i))),
            pl.BlockSpec((None, dm.nkv, tp, hd), lambda i: (batch_of(i), 0, key_block(i), 0)),
        ],
        out_shape=[
            jax.ShapeDtypeStruct((t, mw), BF16),
            jax.ShapeDtypeStruct((dm.b, dm.nkv, hd, dm.lk), BF16),
            jax.ShapeDtypeStruct((dm.b, dm.nkv, dm.lk, hd), BF16),
        ],
        compiler_params=_cparams("parallel"),
        name="attention_prep",
    )(proj, proj, proj, q_norm_g.reshape(1, hd), k_norm_g.reshape(1, hd), *tables)


def _attn_kernel(q_ref, kt_ref, v_ref, o_ref, m_ref, l_ref, acc_ref, kmax_ref, *,
                 hd, n_lat_tiles, lat_keys, ctx_keys):
    tq = q_ref.shape[0]
    i = pl.program_id(2)
    q = q_ref[...]
    qs = jnp.concatenate([q[:, h * hd:(h + 1) * hd] for h in range(GQA_GROUP)], axis=0)

    @pl.when(i == 0)
    def _():
        k = kt_ref[...].astype(F32)
        k2 = jnp.max(jnp.sum(k * k, axis=0, keepdims=True), axis=1, keepdims=True)
        kmax_ref[...] = jnp.broadcast_to(jnp.sqrt(k2), kmax_ref.shape)

    qf = qs.astype(F32)
    shift = jnp.sqrt(jnp.sum(qf * qf, axis=-1, keepdims=True)) * kmax_ref[0:1, 0:1]
    bounded = jnp.max(shift) <= MAX_SAFE_SHIFT

    def finish(l):
        o = acc_ref[...] * (1.0 / l)
        o_ref[...] = jnp.concatenate([o[h * tq:(h + 1) * tq] for h in range(GQA_GROUP)],
                                     axis=1).astype(o_ref.dtype)

    def attend_bounded(tk, nk):
        cb = _largest_divisor(tk, (256, 128))
        m_ref[...] = jnp.broadcast_to(shift, m_ref.shape)
        l_ref[...] = jnp.zeros(l_ref.shape, F32)
        acc_ref[...] = jnp.zeros(acc_ref.shape, F32)

        def body(kt, carry):
            off = kt * tk
            mm = m_ref[...]
            mm = jnp.concatenate([mm] * (cb // 128), axis=1)
            lsum = l_ref[...]
            parts = []
            for c in range(tk // cb):
                col = pl.multiple_of(off + c * cb, cb)
                p = jnp.exp2(jnp.dot(qs, kt_ref[:, pl.ds(col, cb)], preferred_element_type=F32) - mm)
                for u in range(cb // 128):
                    lsum = lsum + p[:, u * 128:(u + 1) * 128]
                parts.append(p.astype(BF16))
            l_ref[...] = lsum
            acc_ref[...] += jnp.dot(jnp.concatenate(parts, axis=1), v_ref[pl.ds(pl.multiple_of(off, tk), tk), :],
                                    preferred_element_type=F32)
            return carry

        lax.fori_loop(0, nk, body, 0, unroll=True)
        finish(jnp.sum(l_ref[...], axis=-1, keepdims=True))

    def attend_online(tk, nk):
        m_ref[...] = jnp.full(m_ref.shape, -jnp.inf, F32)
        l_ref[...] = jnp.zeros(l_ref.shape, F32)
        acc_ref[...] = jnp.zeros(acc_ref.shape, F32)

        def body(kt, carry):
            off = pl.multiple_of(kt * tk, tk)
            s = jnp.dot(qs, kt_ref[:, pl.ds(off, tk)], preferred_element_type=F32)
            m_prev = m_ref[:, 0:1]
            m_new = jnp.maximum(m_prev, jnp.max(s, axis=-1, keepdims=True))
            p = jnp.exp2(s - m_new)
            a = jnp.exp2(m_prev - m_new)
            l_ref[:, 0:1] = a * l_ref[:, 0:1] + jnp.sum(p, axis=-1, keepdims=True)
            acc_ref[...] = a * acc_ref[...] + jnp.dot(p.astype(BF16), v_ref[pl.ds(off, tk), :],
                                                      preferred_element_type=F32)
            m_ref[:, 0:1] = m_new
            return carry

        lax.fori_loop(0, nk, body, 0)
        finish(l_ref[:, 0:1])

    def attend(keys):
        @pl.when(bounded)
        def _():
            attend_bounded(*keys)

        @pl.when(jnp.logical_not(bounded))
        def _():
            attend_online(*keys)

    is_latent = i < n_lat_tiles

    @pl.when(is_latent)
    def _():
        attend(lat_keys)

    if ctx_keys is not None:
        @pl.when(jnp.logical_not(is_latent))
        def _():
            attend(ctx_keys)


def _attention(qp, kt, v, dm, *, with_ctx):
    hd, mw = dm.hd, dm.mw
    tq = _largest_divisor(math.gcd(dm.l, dm.ctx), (256, 128))
    key_tiling = lambda n: (lambda tk: (tk, n // tk))(_largest_divisor(n, (768, 512, 256, 128)))
    gw = GQA_GROUP * hd
    nl, nc = dm.l // tq, dm.ctx // tq
    rows = dm.n_tok if with_ctx else dm.n_lat

    def q_map(b, g, i):
        return jnp.where(i < nl, b * nl + i, dm.n_lat // tq + b * nc + (i - nl)), g

    return pl.pallas_call(
        functools.partial(_attn_kernel, hd=hd, n_lat_tiles=nl, lat_keys=key_tiling(dm.lk),
                          ctx_keys=key_tiling(dm.ctx) if with_ctx else None),
        grid=(dm.b, dm.nkv, nl + (nc if with_ctx else 0)),
        in_specs=[
            pl.BlockSpec((tq, gw), q_map),
            pl.BlockSpec((None, None, hd, dm.lk), lambda b, g, i: (b, g, 0, 0)),
            pl.BlockSpec((None, None, dm.lk, hd), lambda b, g, i: (b, g, 0, 0)),
        ],
        out_specs=pl.BlockSpec((tq, gw), q_map),
        out_shape=jax.ShapeDtypeStruct((rows, mw), BF16),
        scratch_shapes=[
            pltpu.VMEM((GQA_GROUP * tq, 128), F32),
            pltpu.VMEM((GQA_GROUP * tq, 128), F32),
            pltpu.VMEM((GQA_GROUP * tq, hd), F32),
            pltpu.VMEM((8, 128), F32),
        ],
        compiler_params=_cparams("parallel", "parallel", "arbitrary"),
        name="attention",
    )(qp, kt, v)


def _sgu_kernel(u_ref, v_ref, g_ref, b_ref, w_ref, bs_ref, o_ref, *, ng, gw, chunk):
    ts = u_ref.shape[0]
    u = _gelu_tanh(u_ref[...].astype(F32))
    v = _ln(_gelu_tanh(v_ref[...].astype(F32))) * g_ref[...] + b_ref[...]
    vb = v.astype(BF16)
    for c in range(ts // chunk):
        rows = slice(c * chunk, (c + 1) * chunk)
        for g in range(ng):
            cols = slice(g * gw, (g + 1) * gw)
            z = jnp.dot(w_ref[g], vb[rows, cols], preferred_element_type=F32) + bs_ref[g]
            o_ref[rows, cols] = (u[rows, cols] * z).astype(o_ref.dtype)


def _sgu(proj, ln_g, ln_b, w_s, b_s, dm, rows):
    mw, chunk, ng = dm.mw, dm.sgu_chunk, dm.sgu_groups
    ts = _largest_divisor(math.gcd(dm.l, dm.n_ctx), (256, 128))
    return pl.pallas_call(
        functools.partial(_sgu_kernel, ng=ng, gw=mw // ng, chunk=chunk),
        grid=(rows // ts,),
        in_specs=[
            pl.BlockSpec((ts, mw), lambda i: (i, dm.c_u // mw)),
            pl.BlockSpec((ts, mw), lambda i: (i, dm.c_v // mw)),
            pl.BlockSpec((1, mw), lambda i: (0, 0)),
            pl.BlockSpec((1, mw), lambda i: (0, 0)),
            pl.BlockSpec((ng, chunk, chunk), lambda i: (0, 0, 0)),
            pl.BlockSpec((ng, chunk, 1), lambda i: (0, 0, 0)),
        ],
        out_specs=pl.BlockSpec((ts, mw), lambda i: (i, 0)),
        out_shape=jax.ShapeDtypeStruct((rows, mw), BF16),
        compiler_params=_cparams("parallel"),
        name="sgu",
    )(proj, proj, ln_g.reshape(1, mw), ln_b.reshape(1, mw), w_s.astype(BF16), b_s.reshape(ng, chunk, 1))


def _ret_tables(decay_logit, dm, backward):
    c, hd, nh = dm.ret_chunk, dm.hd, dm.ret_heads
    lg = jax.nn.log_sigmoid(decay_logit.astype(F32))
    pos = jnp.arange(c, dtype=F32)
    diff = pos[:, None] - pos[None, :]
    if backward:
        diff = -diff
        xi_e = c - pos
        zeta_e = pos
    else:
        xi_e = pos + 1.0
        zeta_e = c - 1.0 - pos
    dmat = jnp.where(diff >= 0, jnp.exp(jnp.maximum(diff, 0.0)[None] * lg[:, None, None]), 0.0)
    dmat = dmat * hd ** -0.5
    xi = jnp.exp(xi_e[None, :] * lg[:, None])[..., None]
    zeta = (jnp.exp(zeta_e[None, :] * lg[:, None]) * hd ** -0.5)[..., None]
    dec = jnp.broadcast_to(jnp.exp(c * lg)[:, None, None], (nh, 1, hd))
    return dmat.astype(F32), xi.astype(F32), zeta.astype(F32), dec.astype(F32)


def _ret_kernel(*refs, nh, hd, final):
    if final:
        q_ref, k_ref, v_ref, dm_ref, xi_ref, zeta_ref, dec_ref, of_ref, g_ref, o_ref, st_ref = refs
    else:
        q_ref, k_ref, v_ref, dm_ref, xi_ref, zeta_ref, dec_ref, o_ref, st_ref = refs

    @pl.when(pl.program_id(1) == 0)
    def _():
        st_ref[...] = jnp.zeros(st_ref.shape, F32)

    for h in range(nh):
        sl = slice(h * hd, (h + 1) * hd)
        qh, kh, vh = q_ref[:, sl], k_ref[:, sl], v_ref[:, sl]
        s = lax.dot_general(qh, kh, (((1,), (1,)), ((), ())), preferred_element_type=F32)
        inner = (s * dm_ref[h]).astype(BF16)
        st = st_ref[h]
        o = jnp.dot(inner, vh, preferred_element_type=F32)
        o = o + jnp.dot(qh, st.astype(BF16), preferred_element_type=F32) * xi_ref[h]
        kz = (kh.astype(F32) * zeta_ref[h]).astype(BF16)
        st_ref[h] = st * dec_ref[h] + lax.dot_general(kz, vh, (((0,), (0,)), ((), ())),
                                                      preferred_element_type=F32)
        if final:
            gn = _ln(of_ref[:, sl] + o)
            o_ref[:, sl] = (_silu(g_ref[:, sl].astype(F32)) * gn).astype(o_ref.dtype)
        else:
            o_ref[:, sl] = o


def _retention_pass(proj, tables, dm, *, backward, o_fwd=None):
    c, mw, nh, hd = dm.ret_chunk, dm.mw, dm.ret_heads, dm.hd
    n_cc, n_lc = dm.ctx // c, dm.l // c
    steps = n_cc + n_lc

    def row_block(b, t):
        if backward:
            ctx = dm.n_lat // c + b * n_cc + (n_cc - 1 - t)
            lat = b * n_lc + (n_lc - 1 - (t - n_cc))
        else:
            ctx = dm.n_lat // c + b * n_cc + t
            lat = b * n_lc + (t - n_cc)
        return jnp.where(t < n_cc, ctx, lat)

    col = lambda off: (lambda b, t: (row_block(b, t), off // mw))
    const3 = lambda b, t: (0, 0, 0)
    in_specs = [
        pl.BlockSpec((c, mw), col(dm.c_rq)),
        pl.BlockSpec((c, mw), col(dm.c_rk)),
        pl.BlockSpec((c, mw), col(dm.c_rv)),
        pl.BlockSpec((nh, c, c), const3),
        pl.BlockSpec((nh, c, 1), const3),
        pl.BlockSpec((nh, c, 1), const3),
        pl.BlockSpec((nh, 1, hd), const3),
    ]
    args = [proj, proj, proj, *tables]
    final = o_fwd is not None
    if final:
        in_specs += [pl.BlockSpec((c, mw), col(0)), pl.BlockSpec((c, mw), col(dm.c_rg))]
        args += [o_fwd, proj]
    return pl.pallas_call(
        functools.partial(_ret_kernel, nh=nh, hd=hd, final=final),
        grid=(dm.b, steps),
        in_specs=in_specs,
        out_specs=pl.BlockSpec((c, mw), col(0)),
        out_shape=jax.ShapeDtypeStruct((dm.n_tok, mw), BF16 if final else F32),
        scratch_shapes=[pltpu.VMEM((nh, hd, hd), F32)],
        compiler_params=_cparams("parallel", "arbitrary"),
        name="retention_bwd" if final else "retention_fwd",
    )(*args)


def _merge_kernel(att_ref, sgu_ref, ret_ref, ga_ref, gs_ref, gr_ref, wa_ref, ws_ref, wr_ref, wo_ref,
                  x_ref, g1_ref, lng_ref, lnb_ref, o_ref, *, alpha):
    def branch(a_ref, w_ref, gate_ref):
        return jax.nn.sigmoid(gate_ref[...].astype(F32)) * jnp.dot(a_ref[...], w_ref[...],
                                                                    preferred_element_type=F32)

    m = branch(att_ref, wa_ref, ga_ref) + branch(sgu_ref, ws_ref, gs_ref) + branch(ret_ref, wr_ref, gr_ref)
    mix = jnp.dot(m.astype(BF16), wo_ref[...], preferred_element_type=F32)
    y = alpha * x_ref[...] + g1_ref[...] * mix
    o_ref[...] = _ln(y) * lng_ref[...] + lnb_ref[...]


def _merge(att, sgu, ret, proj, w_att, w_sgu, w_ret, w_o, xs, mods, ln_g, ln_b, dm, layer, rows):
    d, mw = dm.d, dm.mw
    tm = _largest_divisor(math.gcd(dm.l, dm.n_ctx), (256, 128))
    row = dm.mod_row
    tok = lambda w: pl.BlockSpec((tm, w), lambda i: (i, 0))
    gate = lambda k: pl.BlockSpec((tm, d), lambda i: (i, dm.c_bg // d + k))
    vec = pl.BlockSpec((1, d), lambda i: (0, 0))
    return pl.pallas_call(
        functools.partial(_merge_kernel, alpha=dm.alpha),
        grid=(rows // tm,),
        in_specs=[
            tok(mw), tok(mw), tok(mw), gate(0), gate(1), gate(2),
            _resident((mw, d), lambda i: (0, 0)), _resident((mw, d), lambda i: (0, 0)),
            _resident((mw, d), lambda i: (0, 0)), _resident((d, d), lambda i: (0, 0)),
            tok(d),
            pl.BlockSpec((None, 1, d), lambda i: (row(layer, 2)(i, tm), 0, 0)),
            vec, vec,
        ],
        out_specs=tok(d),
        out_shape=jax.ShapeDtypeStruct((rows, d), F32),
        compiler_params=_cparams("parallel"),
        name="merge",
    )(att, sgu, ret, proj, proj, proj, w_att, w_sgu, w_ret, w_o, xs, mods,
      ln_g.reshape(1, d), ln_b.reshape(1, d))


def _ffn_kernel(x_ref, sh_ref, sc_ref, w1_ref, w3_ref, w2_ref, g2_ref, lng_ref, lnb_ref, o_ref,
                h_ref, acc_ref, *, alpha):
    j = pl.program_id(1)

    @pl.when(j == 0)
    def _():
        h = _ln(x_ref[...]) * (1.0 + sc_ref[...]) + sh_ref[...]
        h_ref[...] = h.astype(BF16)
        acc_ref[...] = jnp.zeros(acc_ref.shape, F32)

    h = h_ref[...]
    a = jnp.dot(h, w1_ref[...], preferred_element_type=F32)
    b = jnp.dot(h, w3_ref[...], preferred_element_type=F32)
    acc_ref[...] += jnp.dot((_silu(a) * b).astype(BF16), w2_ref[...], preferred_element_type=F32)

    @pl.when(j == pl.num_programs(1) - 1)
    def _():
        y = alpha * x_ref[...] + g2_ref[...] * acc_ref[...]
        o_ref[...] = _ln(y) * lng_ref[...] + lnb_ref[...]


def _ffn(xs, mods, w1, w3, w2, ln_g, ln_b, dm, layer):
    t, d = xs.shape
    f = w1.shape[1]
    tm = _largest_divisor(math.gcd(dm.l, dm.n_ctx), (512, 256, 128))
    tf = _largest_divisor(f, (512, 256, 128))
    row = dm.mod_row
    mod = lambda k: pl.BlockSpec((None, 1, d), lambda i, j: (row(layer, k)(i, tm), 0, 0))
    vec = pl.BlockSpec((1, d), lambda i, j: (0, 0))
    return pl.pallas_call(
        functools.partial(_ffn_kernel, alpha=dm.alpha),
        grid=(t // tm, f // tf),
        in_specs=[
            pl.BlockSpec((tm, d), lambda i, j: (i, 0)),
            mod(3), mod(4),
            pl.BlockSpec((d, tf), lambda i, j: (0, j)),
            pl.BlockSpec((d, tf), lambda i, j: (0, j)),
            pl.BlockSpec((tf, d), lambda i, j: (j, 0)),
            mod(5), vec, vec,
        ],
        out_specs=pl.BlockSpec((tm, d), lambda i, j: (i, 0)),
        out_shape=jax.ShapeDtypeStruct((t, d), F32),
        scratch_shapes=[pltpu.VMEM((tm, d), BF16), pltpu.VMEM((tm, d), F32)],
        compiler_params=_cparams("parallel", "arbitrary"),
        name="ffn_dense",
    )(xs, mods, mods, w1, w3, w2, mods, ln_g.reshape(1, d), ln_b.reshape(1, d))


def _router_kernel(x_ref, sh_ref, sc_ref, whi_ref, wlo_ref, tril_ref, h_ref, gates_ref, code_ref, rank_ref,
                   count_ref):
    h = _ln(x_ref[...]) * (1.0 + sc_ref[...]) + sh_ref[...]
    h_ref[...] = h
    h_hi = h.astype(BF16)
    h_lo = (h - h_hi.astype(F32)).astype(BF16)
    w_hi = whi_ref[...]
    logits = (jnp.dot(h_hi, w_hi, preferred_element_type=F32)
              + jnp.dot(h_lo, w_hi, preferred_element_type=F32)
              + jnp.dot(h_hi, wlo_ref[...], preferred_element_type=F32))
    n_exp = float(logits.shape[-1])
    ids = lax.broadcasted_iota(jnp.int32, logits.shape, 1).astype(F32)
    m1 = jnp.max(logits, axis=-1, keepdims=True)
    i1 = jnp.min(jnp.where(logits == m1, ids, n_exp), axis=-1, keepdims=True)
    rest = jnp.where(ids == i1, -jnp.inf, logits)
    m2 = jnp.max(rest, axis=-1, keepdims=True)
    i2 = jnp.min(jnp.where(rest == m2, ids, n_exp), axis=-1, keepdims=True)
    e = jnp.exp(m2 - m1)
    w1 = 1.0 / (1.0 + e)
    w2 = e * w1
    gates_ref[...] = jnp.where(ids == i1, w1, jnp.where(ids == i2, w2, 0.0))
    code_ref[...] = jnp.where(ids == i1, 1, jnp.where(ids == i2, 2, 0)).astype(jnp.int32)

    @pl.when(pl.program_id(0) == 0)
    def _():
        count_ref[...] = jnp.zeros(count_ref.shape, F32)

    chosen = jnp.where(jnp.logical_or(ids == i1, ids == i2), 1.0, 0.0)
    before = jnp.dot(tril_ref[...], chosen.astype(BF16), preferred_element_type=F32) + count_ref[...]
    rank_ref[...] = before.astype(jnp.int32)
    count_ref[...] = count_ref[...] + jnp.sum(chosen, axis=0, keepdims=True)


def _router(xs, mods, router_w, dm, layer, rows):
    d, n_exp = router_w.shape
    tm = _largest_divisor(dm.l, (512, 256, 128))
    row = dm.mod_row
    mod = lambda k: pl.BlockSpec((None, 1, d), lambda i: (row(layer, k)(i, tm), 0, 0))
    w_hi = router_w.astype(BF16)
    w_lo = (router_w - w_hi.astype(F32)).astype(BF16)
    return pl.pallas_call(
        _router_kernel,
        grid=(rows // tm,),
        in_specs=[
            pl.BlockSpec((tm, d), lambda i: (i, 0)), mod(3), mod(4),
            pl.BlockSpec((d, n_exp), lambda i: (0, 0)),
            pl.BlockSpec((d, n_exp), lambda i: (0, 0)),
            pl.BlockSpec((tm, tm), lambda i: (0, 0)),
        ],
        out_specs=[
            pl.BlockSpec((tm, d), lambda i: (i, 0)),
            pl.BlockSpec((tm, n_exp), lambda i: (i, 0)),
            pl.BlockSpec((tm, n_exp), lambda i: (i, 0)),
            pl.BlockSpec((tm, n_exp), lambda i: (i, 0)),
        ],
        out_shape=[
            jax.ShapeDtypeStruct((rows, d), F32),
            jax.ShapeDtypeStruct((rows, n_exp), F32),
            jax.ShapeDtypeStruct((rows, n_exp), jnp.int32),
            jax.ShapeDtypeStruct((rows, n_exp), jnp.int32),
        ],
        scratch_shapes=[pltpu.VMEM((1, n_exp), F32)],
        compiler_params=_cparams("arbitrary"),
        name="router",
    )(xs, mods, mods, w_hi, w_lo, jnp.tril(jnp.ones((tm, tm), BF16), -1))


def _row(ref, t):
    return ref.at[pl.ds(t, 1), :]


def _row_gather(index_of, lo, hi, hbm_ref, buf_ref, sem, *, wait):
    def body(q, carry):
        for u in range(ROW_DMA_UNROLL):
            r = lo + q * ROW_DMA_UNROLL + u
            cp = pltpu.make_async_copy(_row(hbm_ref, index_of(r)), _row(buf_ref, r), sem)
            if wait:
                cp.wait()
            else:
                cp.start(priority=0)
        return carry

    lax.fori_loop(0, (hi - lo) // ROW_DMA_UNROLL, body, 0)


def _row_scatter(index_of, lo, hi, buf_ref, hbm_ref, sem, *, wait):
    def body(r, carry):
        cp = pltpu.make_async_copy(_row(buf_ref, r), _row(hbm_ref, index_of(r)), sem)
        if wait:
            cp.wait()
        else:
            cp.start(priority=1)
        return carry

    lax.fori_loop(lo, hi, body, 0)


def _expert_kernel(te_ref, tv_ref, tok_cur, tok_nxt, dst_prv, h_hbm, w1_ref, w3_ref, w2_ref, y_hbm,
                   in_buf, out_buf, in_sem, out_sem, h_ref, acc_ref, *, nj):
    i, j = pl.program_id(0), pl.program_id(1)
    nt = pl.num_programs(0)
    tm = h_ref.shape[0]
    slot = i % 2
    chunk = -(-tm // (nj * ROW_DMA_UNROLL)) * ROW_DMA_UNROLL
    lo, hi = jnp.minimum(j * chunk, tm), jnp.minimum((j + 1) * chunk, tm)
    valid = tv_ref[i] > 0
    prev_rows = jnp.where(i > 0, tv_ref[jnp.maximum(i - 1, 0)], 0)
    prev_valid = prev_rows > 0
    next_valid = jnp.logical_and(i + 1 < nt, tv_ref[jnp.minimum(i + 1, nt - 1)] > 0)
    gather = functools.partial(_row_gather, hbm_ref=h_hbm)
    scatter = functools.partial(_row_scatter, lambda r: dst_prv[0, 0, r], buf_ref=out_buf.at[1 - slot],
                                hbm_ref=y_hbm, sem=out_sem.at[1 - slot])

    @pl.when(jnp.logical_and(valid, j == 0))
    def _():
        @pl.when(i == 0)
        def _():
            gather(lambda r: tok_cur[0, 0, r], 0, tm, buf_ref=in_buf.at[0], sem=in_sem.at[0], wait=False)

        gather(lambda r: tok_cur[0, 0, r], 0, tm, buf_ref=in_buf.at[slot], sem=in_sem.at[slot], wait=True)
        h_ref[...] = in_buf[slot].astype(BF16)
        acc_ref[...] = jnp.zeros(acc_ref.shape, F32)

    @pl.when(next_valid)
    def _():
        gather(lambda r: tok_nxt[0, 0, r], lo, hi, buf_ref=in_buf.at[1 - slot], sem=in_sem.at[1 - slot], wait=False)

    @pl.when(prev_valid)
    def _():
        scatter(jnp.minimum(lo, prev_rows), jnp.minimum(hi, prev_rows), wait=False)

    @pl.when(valid)
    def _():
        h = h_ref[...]
        a = jnp.dot(h, w1_ref[...], preferred_element_type=F32)
        b = jnp.dot(h, w3_ref[...], preferred_element_type=F32)
        acc_ref[...] += jnp.dot((_silu(a) * b).astype(BF16), w2_ref[...], preferred_element_type=F32)

        @pl.when(j == nj - 1)
        def _():
            out_buf[slot] = acc_ref[...]

    @pl.when(jnp.logical_and(prev_valid, j == nj - 1))
    def _():
        scatter(0, prev_rows, wait=True)


def _expert_mlp(h_tm, slot_token, slot_dest, tile_expert, tile_valid, w1, w3, w2, tm, d, n_out):
    n_slots = slot_token.shape[0]
    f = w1.shape[2]
    tf = _largest_divisor(f, (1024, 512, 256, 128))
    nj, nt = f // tf, n_slots // tm
    jj = lambda i, j, tv: jnp.where(tv[i] > 0, j, nj - 1)
    tok = slot_token.reshape(nt, 1, tm)
    dst = slot_dest.reshape(nt, 1, tm)
    smem = lambda f_i: pl.BlockSpec((1, 1, tm), lambda i, j, te, tv: (f_i(i), 0, 0), memory_space=pltpu.SMEM)
    grid_spec = pltpu.PrefetchScalarGridSpec(
        num_scalar_prefetch=2,
        grid=(nt, nj),
        in_specs=[
            smem(lambda i: i),
            smem(lambda i: jnp.minimum(i + 1, nt - 1)),
            smem(lambda i: jnp.maximum(i - 1, 0)),
            pl.BlockSpec(memory_space=pl.ANY),
            pl.BlockSpec((None, d, tf), lambda i, j, te, tv: (te[i], 0, jj(i, j, tv))),
            pl.BlockSpec((None, d, tf), lambda i, j, te, tv: (te[i], 0, jj(i, j, tv))),
            pl.BlockSpec((None, tf, d), lambda i, j, te, tv: (te[i], jj(i, j, tv), 0)),
        ],
        out_specs=pl.BlockSpec(memory_space=pl.ANY),
        scratch_shapes=[
            pltpu.VMEM((2, tm, d), F32),
            pltpu.VMEM((2, tm, d), F32),
            pltpu.SemaphoreType.DMA((2,)),
            pltpu.SemaphoreType.DMA((2,)),
            pltpu.VMEM((tm, d), BF16),
            pltpu.VMEM((tm, d), F32),
        ],
    )
    return pl.pallas_call(
        functools.partial(_expert_kernel, nj=nj),
        grid_spec=grid_spec,
        out_shape=jax.ShapeDtypeStruct((n_out, d), F32),
        compiler_params=_cparams("arbitrary", "arbitrary"),
        name="expert_mlp",
    )(tile_expert, tile_valid, tok, tok, dst, h_tm, w1, w3, w2)


def _moe_out_kernel(y1_ref, y2_ref, gates_ref, code_ref, x_ref, g2_ref, lng_ref, lnb_ref, o_ref, *, alpha):
    tm, d = x_ref.shape
    gates, code = gates_ref[...], code_ref[...]
    f = jnp.zeros((tm, d), F32)
    for k, y_ref in enumerate((y1_ref, y2_ref)):
        w = jnp.sum(jnp.where(code == k + 1, gates, 0.0), axis=-1, keepdims=True)
        f = f + w * y_ref[...]
    y = alpha * x_ref[...] + g2_ref[...] * f
    o_ref[...] = _ln(y) * lng_ref[...] + lnb_ref[...]


def _moe_out(y_tm, gates, code, xs, mods, ln_g, ln_b, dm, layer, rows):
    d, n_exp = dm.d, gates.shape[1]
    tm = _largest_divisor(dm.l, (256, 128))
    nt = rows // tm
    row = dm.mod_row
    vec = pl.BlockSpec((1, d), lambda i: (0, 0))
    return pl.pallas_call(
        functools.partial(_moe_out_kernel, alpha=dm.alpha),
        grid=(nt,),
        in_specs=[
            pl.BlockSpec((tm, d), lambda i: (i, 0)),
            pl.BlockSpec((tm, d), lambda i: (nt + i, 0)),
            pl.BlockSpec((tm, n_exp), lambda i: (i, 0)),
            pl.BlockSpec((tm, n_exp), lambda i: (i, 0)),
            pl.BlockSpec((tm, d), lambda i: (i, 0)),
            pl.BlockSpec((None, 1, d), lambda i: (row(layer, 5)(i, tm), 0, 0)),
            vec, vec,
        ],
        out_specs=pl.BlockSpec((tm, d), lambda i: (i, 0)),
        out_shape=jax.ShapeDtypeStruct((rows, d), F32),
        compiler_params=_cparams("parallel"),
        name="moe_out",
    )(y_tm, y_tm, gates, code, xs, mods, ln_g.reshape(1, d), ln_b.reshape(1, d))


def _moe(xs, mods, router_w, w1, w3, w2, ln_g, ln_b, dm, layer, rows):
    n_exp = router_w.shape[1]
    tm = 512 if rows * TOP_K >= 512 * n_exp * 4 else 128
    h, gates, code, rank = _router(xs, mods, router_w, dm, layer, rows)

    counts = jnp.sum((code > 0).astype(jnp.int32), axis=0)
    padded = ((counts + tm - 1) // tm) * tm
    ends = jnp.cumsum(padded)
    starts = ends - padded
    slot_of = starts[None, :] + rank
    slot = jnp.concatenate([jnp.sum(jnp.where(code == k + 1, slot_of, 0), axis=1) for k in range(TOP_K)])
    n_slots = TOP_K * rows + (n_exp + 1) * tm
    assignment = jnp.arange(TOP_K * rows, dtype=jnp.int32)
    slot_dest = jnp.full((n_slots,), -1, jnp.int32).at[slot].set(assignment)
    slot_token = jnp.maximum(slot_dest, 0) % rows
    tile_start = jnp.arange(n_slots // tm, dtype=jnp.int32) * tm
    last_used = jnp.minimum(tile_start, ends[-1] - 1)
    tile_expert = jnp.minimum(jnp.sum((ends[None, :] <= last_used[:, None]).astype(jnp.int32), axis=1),
                              n_exp - 1).astype(jnp.int32)
    tile_rows = jnp.where(tile_start < ends[-1],
                          jnp.clip((starts + counts)[tile_expert] - tile_start, 0, tm), 0).astype(jnp.int32)

    y = _expert_mlp(h, slot_token, slot_dest, tile_expert, tile_rows, w1, w3, w2, tm, dm.d, TOP_K * rows)
    return _moe_out(y, gates, code, xs, mods, ln_g, ln_b, dm, layer, rows)


def kernel(x, c, ctx, c_ctx, ada_w, ada_b, w_in, q_norm_g, k_norm_g, sgu_ln_g, sgu_ln_b, sgu_w, sgu_b,
           ret_decay_fwd, ret_decay_bwd, w_br_att, w_br_sgu, w_br_ret, w_out, ln1_g, ln1_b, ln2_g, ln2_b,
           ffn_w1, ffn_w3, ffn_w2, router_w, moe_w1, moe_w3, moe_w2):
    dm = _Dims(x, ctx, ada_w, q_norm_g, sgu_w, ret_decay_fwd, ffn_w1, moe_w1)
    assert dm.b + 1 <= MOD_ROWS and dm.ret_heads == dm.nq
    d = dm.d

    cond = jnp.concatenate([c, c_ctx[None], jnp.zeros((MOD_ROWS - dm.b - 1, d), F32)], axis=0)
    mods = _ada_mods(cond, ada_w, ada_b)
    rope = _rope_tables(dm)

    xs = jnp.concatenate([x.reshape(dm.n_lat, d), ctx.reshape(dm.n_ctx, d)], axis=0)
    for i in range(dm.depth):
        last = i == dm.depth - 1
        rows = dm.n_lat if last else dm.n_tok

        proj = _in_projection(xs, mods, _reorder_w_in(w_in[i], dm), dm, i)

        qp, kt, v_all = _attention_prep(proj, q_norm_g[i], k_norm_g[i], rope, dm)
        att = _attention(qp, kt, v_all, dm, with_ctx=not last)

        sgu = _sgu(proj, sgu_ln_g[i], sgu_ln_b[i], sgu_w[i], sgu_b[i], dm, rows)

        o_fwd = _retention_pass(proj, _ret_tables(ret_decay_fwd[i], dm, False), dm, backward=False)
        ret = _retention_pass(proj, _ret_tables(ret_decay_bwd[i], dm, True), dm, backward=True, o_fwd=o_fwd)

        xs1 = _merge(att, sgu, ret, proj, w_br_att[i].astype(BF16), w_br_sgu[i].astype(BF16),
                     w_br_ret[i].astype(BF16), w_out[i].astype(BF16), xs, mods, ln1_g[i], ln1_b[i], dm, i, rows)

        j = i // 2
        if i % 2 == 0:
            assert rows == xs1.shape[0]
            xs = _ffn(xs1, mods, ffn_w1[j].astype(BF16), ffn_w3[j].astype(BF16), ffn_w2[j].astype(BF16),
                      ln2_g[i], ln2_b[i], dm, i)
        else:
            xs = _moe(xs1, mods, router_w[j], moe_w1[j].astype(BF16), moe_w3[j].astype(BF16),
                      moe_w2[j].astype(BF16), ln2_g[i], ln2_b[i], dm, i, rows)
    return xs[:dm.n_lat].reshape(dm.b, dm.l, d)
```

```python
import functools
import math

import jax
import jax.numpy as jnp
from jax import lax
from jax.experimental import pallas as pl
from jax.experimental.pallas import tpu as pltpu

F32 = jnp.float32
BF16 = jnp.bfloat16
EPS = 1e-6
GRID_W = 64
ROPE_THETA = 10000.0
GQA_GROUP = 4
TOP_K = 2
N_MODS = 6
MOD_ROWS = 8
LANES = 128
ROW_DMA_UNROLL = 4
V7X_VMEM_BYTES = 64 * 1024 * 1024
VMEM_LIMIT = V7X_VMEM_BYTES - 8 * 1024 * 1024
LOG2E = 1.4426950408889634
MAX_SAFE_SHIFT = 60.0


def _cparams(*sem):
    return pltpu.CompilerParams(dimension_semantics=sem, vmem_limit_bytes=VMEM_LIMIT)


def _resident(shape, index_map):
    return pl.BlockSpec(shape, index_map, pipeline_mode=pl.Buffered(1))


def _largest_divisor(n, candidates):
    for c in candidates:
        if n % c == 0:
            return c
    raise ValueError(f"no tile in {candidates} divides {n}")


def _ln(x):
    mu = jnp.mean(x, axis=-1, keepdims=True)
    xc = x - mu
    var = jnp.mean(xc * xc, axis=-1, keepdims=True)
    return xc * lax.rsqrt(var + EPS)


def _silu(x):
    return x * jax.nn.sigmoid(x)


def _gelu_tanh(x):
    return 0.5 * x * (1.0 + jnp.tanh(math.sqrt(2.0 / math.pi) * (x + 0.044715 * (x * x * x))))


def _store_token_major(ref, x):
    t, d = x.shape
    g = d // LANES
    for c in range(g):
        ref[pl.ds(c, t, stride=g), :] = x[:, c * LANES:(c + 1) * LANES]


def _load_token_major(ref, t, g):
    return jnp.concatenate([ref[pl.ds(c, t, stride=g), :] for c in range(g)], axis=1)


def _ada_kernel(c_ref, w_ref, b_ref, o_ref):
    s = _silu(c_ref[...]).astype(BF16)
    o_ref[...] = jnp.dot(s, w_ref[...].astype(BF16), preferred_element_type=F32) + b_ref[...]


def _ada_mods(cond, ada_w, ada_b):
    depth, d, nd = ada_w.shape
    tn = _largest_divisor(nd, (1024, 512, 256, 128))
    out = pl.pallas_call(
        _ada_kernel,
        grid=(depth, nd // tn),
        in_specs=[
            pl.BlockSpec((MOD_ROWS, d), lambda l, j: (0, 0)),
            pl.BlockSpec((None, d, tn), lambda l, j: (l, 0, j)),
            pl.BlockSpec((None, 1, tn), lambda l, j: (l, 0, j)),
        ],
        out_specs=pl.BlockSpec((None, MOD_ROWS, tn), lambda l, j: (l, 0, j)),
        out_shape=jax.ShapeDtypeStruct((depth, MOD_ROWS, nd), F32),
        compiler_params=_cparams("parallel", "parallel"),
        name="ada_mods",
    )(cond, ada_w, ada_b.reshape(depth, 1, nd))
    return out.reshape(depth * MOD_ROWS * N_MODS, 1, d)


class _Dims:
    def __init__(self, x, ctx, ada_w, q_norm_g, sgu_w, ret_decay_fwd, ffn_w1, moe_w1):
        self.b, self.l, self.d = x.shape
        self.ctx = ctx.shape[1]
        self.depth = ada_w.shape[0]
        self.hd = q_norm_g.shape[1]
        self.mw = self.d // 2
        self.nq = self.mw // self.hd
        self.nkv = self.nq // GQA_GROUP
        self.kvw = self.nkv * self.hd
        self.sgu_groups, self.sgu_chunk = sgu_w.shape[1], sgu_w.shape[2]
        self.ret_heads = ret_decay_fwd.shape[1]
        self.ret_chunk = _largest_divisor(math.gcd(self.l, self.ctx), (256, 128))
        self.n_lat = self.b * self.l
        self.n_ctx = self.b * self.ctx
        self.n_tok = self.n_lat + self.n_ctx
        self.lk = self.ctx + self.l
        self.n_exp = moe_w1.shape[1]
        self.alpha = (2 * self.depth) ** 0.25
        mw = self.mw
        self.c_u, self.c_v = 0, mw
        self.c_rq, self.c_rk, self.c_rv, self.c_rg = 2 * mw, 3 * mw, 4 * mw, 5 * mw
        self.c_bg = 6 * mw
        self.c_aq = 6 * mw + 3 * self.d
        self.c_ak = self.c_aq + mw
        self.c_av = self.c_ak + self.kvw
        self.n_cols = self.c_av + self.kvw

    def mod_row(self, layer, j):
        def f(i, tm):
            bidx = jnp.minimum((i * tm) // self.l, self.b)
            return (layer * MOD_ROWS + bidx) * N_MODS + j
        return f


def _reorder_w_in(w, dm):
    mw, kvw = dm.mw, dm.kvw
    aq = w[:, :mw]
    ak = w[:, mw:mw + kvw]
    av = w[:, mw + kvw:mw + 2 * kvw]
    rest = w[:, mw + 2 * kvw:]
    return jnp.concatenate([rest, aq, ak, av], axis=1).astype(BF16)


def _proj_kernel(x_ref, sh_ref, sc_ref, w_ref, o_ref, h_ref):
    @pl.when(pl.program_id(1) == 0)
    def _():
        h = _ln(x_ref[...]) * (1.0 + sc_ref[...]) + sh_ref[...]
        h_ref[...] = h.astype(BF16)

    o_ref[...] = jnp.dot(h_ref[...], w_ref[...], preferred_element_type=F32).astype(o_ref.dtype)


def _in_projection(xs, mods, w, dm, layer):
    t, d = xs.shape
    nc = w.shape[1]
    tm = _largest_divisor(math.gcd(dm.l, dm.n_ctx), (1024, 512, 256, 128))
    tn = _largest_divisor(nc, (3 * dm.mw // 2, dm.mw // 2))
    row = dm.mod_row
    return pl.pallas_call(
        _proj_kernel,
        grid=(t // tm, nc // tn),
        in_specs=[
            pl.BlockSpec((tm, d), lambda i, j: (i, 0)),
            pl.BlockSpec((None, 1, d), lambda i, j: (row(layer, 0)(i, tm), 0, 0)),
            pl.BlockSpec((None, 1, d), lambda i, j: (row(layer, 1)(i, tm), 0, 0)),
            pl.BlockSpec((d, tn), lambda i, j: (0, j)),
        ],
        out_specs=pl.BlockSpec((tm, tn), lambda i, j: (i, j)),
        out_shape=jax.ShapeDtypeStruct((t, nc), BF16),
        scratch_shapes=[pltpu.VMEM((tm, d), BF16)],
        compiler_params=_cparams("parallel", "arbitrary"),
        name="in_projection",
    )(xs, mods, mods, w)


def _rope_tables(dm):
    n_freq = dm.hd // 4
    t = jnp.arange(dm.l, dtype=F32)
    row = jnp.floor(t / GRID_W)
    col = t - row * GRID_W
    inv_freq = ROPE_THETA ** (-jnp.arange(n_freq, dtype=F32) / n_freq)
    ar, ac = row[:, None] * inv_freq, col[:, None] * inv_freq
    z = jnp.zeros_like(ar)
    cos = jnp.concatenate([jnp.cos(ar), jnp.cos(ar), jnp.cos(ac), jnp.cos(ac)], axis=1)
    sin_lo = jnp.concatenate([z, jnp.sin(ar), z, jnp.sin(ac)], axis=1)
    sin_hi = jnp.concatenate([-jnp.sin(ar), z, -jnp.sin(ac), z], axis=1)
    pad = lambda a, v: jnp.concatenate([a, jnp.full((dm.ctx, dm.hd), v, F32)], axis=0)
    return pad(cos, 1.0), pad(sin_lo, 0.0), pad(sin_hi, 0.0)


def _prep_kernel(q_ref, k_ref, v_ref, qg_ref, kg_ref, cos_ref, slo_ref, shi_ref, qo_ref, kt_ref, vo_ref,
                 *, nq, nkv, hd, qscale):
    cos, slo, shi = cos_ref[...], slo_ref[...], shi_ref[...]
    quarter = hd // 4

    def norm_rope(xh, g):
        y = xh * lax.rsqrt(jnp.mean(xh * xh, axis=-1, keepdims=True) + EPS) * g
        return y * cos + pltpu.roll(y, quarter, 1) * slo + pltpu.roll(y, hd - quarter, 1) * shi

    qg = qg_ref[...] * qscale
    kg = kg_ref[...]
    for h in range(nq):
        sl = slice(h * hd, (h + 1) * hd)
        qo_ref[:, sl] = norm_rope(q_ref[:, sl].astype(F32), qg).astype(qo_ref.dtype)
    for g in range(nkv):
        sl = slice(g * hd, (g + 1) * hd)
        kt_ref[g] = norm_rope(k_ref[:, sl].astype(F32), kg).T.astype(kt_ref.dtype)
        vo_ref[g] = v_ref[:, sl]


def _attention_prep(proj, q_norm_g, k_norm_g, tables, dm):
    t = proj.shape[0]
    tp = _largest_divisor(math.gcd(dm.l, dm.ctx), (256, 128))
    mw, kvw, hd = dm.mw, dm.kvw, dm.hd
    n_lat_tiles = dm.n_lat // tp

    def batch_of(i):
        return jnp.where(i < n_lat_tiles, (i * tp) // dm.l, (i * tp - dm.n_lat) // dm.ctx)

    def key_block(i):
        lat = (dm.ctx + (i * tp) % dm.l) // tp
        ctx = ((i * tp - dm.n_lat) % dm.ctx) // tp
        return jnp.where(i < n_lat_tiles, lat, ctx)

    def table_block(i):
        lat = ((i * tp) % dm.l) // tp
        ctx = dm.l // tp + ((i * tp - dm.n_lat) % dm.ctx) // tp
        return jnp.where(i < n_lat_tiles, lat, ctx)

    tab_spec = pl.BlockSpec((tp, hd), lambda i: (table_block(i), 0))
    qscale = hd ** -0.5 * LOG2E
    return pl.pallas_call(
        functools.partial(_prep_kernel, nq=dm.nq, nkv=dm.nkv, hd=hd, qscale=qscale),
        grid=(t // tp,),
        in_specs=[
            pl.BlockSpec((tp, mw), lambda i: (i, dm.c_aq // mw)),
            pl.BlockSpec((tp, kvw), lambda i: (i, dm.c_ak // kvw)),
            pl.BlockSpec((tp, kvw), lambda i: (i, dm.c_av // kvw)),
            pl.BlockSpec((1, hd), lambda i: (0, 0)),
            pl.BlockSpec((1, hd), lambda i: (0, 0)),
            tab_spec, tab_spec, tab_spec,
        ],
        out_specs=[
            pl.BlockSpec((tp, mw), lambda i: (i, 0)),
            pl.BlockSpec((None, dm.nkv, hd, tp), lambda i: (batch_of(i), 0, 0, key_block(i))),
            pl.BlockSpec((None, dm.nkv, tp, hd), lambda i: (batch_of(i), 0, key_block(i), 0)),
        ],
        out_shape=[
            jax.ShapeDtypeStruct((t, mw), BF16),
            jax.ShapeDtypeStruct((dm.b, dm.nkv, hd, dm.lk), BF16),
            jax.ShapeDtypeStruct((dm.b, dm.nkv, dm.lk, hd), BF16),
        ],
        compiler_params=_cparams("parallel"),
        name="attention_prep",
    )(proj, proj, proj, q_norm_g.reshape(1, hd), k_norm_g.reshape(1, hd), *tables)


def _attn_kernel(q_ref, kt_ref, v_ref, o_ref, m_ref, l_ref, acc_ref, kmax_ref, *,
                 hd, n_lat_tiles, lat_keys, ctx_keys):
    tq = q_ref.shape[0]
    i = pl.program_id(2)
    q = q_ref[...]
    qs = jnp.concatenate([q[:, h * hd:(h + 1) * hd] for h in range(GQA_GROUP)], axis=0)

    @pl.when(i == 0)
    def _():
        k = kt_ref[...].astype(F32)
        k2 = jnp.max(jnp.sum(k * k, axis=0, keepdims=True), axis=1, keepdims=True)
        kmax_ref[...] = jnp.broadcast_to(jnp.sqrt(k2), kmax_ref.shape)

    qf = qs.astype(F32)
    shift = jnp.sqrt(jnp.sum(qf * qf, axis=-1, keepdims=True)) * kmax_ref[0:1, 0:1]
    bounded = jnp.max(shift) <= MAX_SAFE_SHIFT

    def finish(l):
        o = acc_ref[...] * (1.0 / l)
        o_ref[...] = jnp.concatenate([o[h * tq:(h + 1) * tq] for h in range(GQA_GROUP)],
                                     axis=1).astype(o_ref.dtype)

    def attend_bounded(tk, nk):
        cb = _largest_divisor(tk, (256, 128))
        m_ref[...] = jnp.broadcast_to(shift, m_ref.shape)
        l_ref[...] = jnp.zeros(l_ref.shape, F32)
        acc_ref[...] = jnp.zeros(acc_ref.shape, F32)

        def body(kt, carry):
            off = kt * tk
            mm = m_ref[...]
            mm = jnp.concatenate([mm] * (cb // 128), axis=1)
            lsum = l_ref[...]
            parts = []
            for c in range(tk // cb):
                col = pl.multiple_of(off + c * cb, cb)
                p = jnp.exp2(jnp.dot(qs, kt_ref[:, pl.ds(col, cb)], preferred_element_type=F32) - mm)
                for u in range(cb // 128):
                    lsum = lsum + p[:, u * 128:(u + 1) * 128]
                parts.append(p.astype(BF16))
            l_ref[...] = lsum
            acc_ref[...] += jnp.dot(jnp.concatenate(parts, axis=1), v_ref[pl.ds(pl.multiple_of(off, tk), tk), :],
                                    preferred_element_type=F32)
            return carry

        lax.fori_loop(0, nk, body, 0, unroll=True)
        finish(jnp.sum(l_ref[...], axis=-1, keepdims=True))

    def attend_online(tk, nk):
        m_ref[...] = jnp.full(m_ref.shape, -jnp.inf, F32)
        l_ref[...] = jnp.zeros(l_ref.shape, F32)
        acc_ref[...] = jnp.zeros(acc_ref.shape, F32)

        def body(kt, carry):
            off = pl.multiple_of(kt * tk, tk)
            s = jnp.dot(qs, kt_ref[:, pl.ds(off, tk)], preferred_element_type=F32)
            m_prev = m_ref[:, 0:1]
            m_new = jnp.maximum(m_prev, jnp.max(s, axis=-1, keepdims=True))
            p = jnp.exp2(s - m_new)
            a = jnp.exp2(m_prev - m_new)
            l_ref[:, 0:1] = a * l_ref[:, 0:1] + jnp.sum(p, axis=-1, keepdims=True)
            acc_ref[...] = a * acc_ref[...] + jnp.dot(p.astype(BF16), v_ref[pl.ds(off, tk), :],
                                                      preferred_element_type=F32)
            m_ref[:, 0:1] = m_new
            return carry

        lax.fori_loop(0, nk, body, 0)
        finish(l_ref[:, 0:1])

    def attend(keys):
        @pl.when(bounded)
        def _():
            attend_bounded(*keys)

        @pl.when(jnp.logical_not(bounded))
        def _():
            attend_online(*keys)

    is_latent = i < n_lat_tiles

    @pl.when(is_latent)
    def _():
        attend(lat_keys)

    if ctx_keys is not None:
        @pl.when(jnp.logical_not(is_latent))
        def _():
            attend(ctx_keys)


def _attention(qp, kt, v, dm, *, with_ctx):
    hd, mw = dm.hd, dm.mw
    tq = _largest_divisor(math.gcd(dm.l, dm.ctx), (256, 128))
    key_tiling = lambda n: (lambda tk: (tk, n // tk))(_largest_divisor(n, (768, 512, 256, 128)))
    gw = GQA_GROUP * hd
    nl, nc = dm.l // tq, dm.ctx // tq
    rows = dm.n_tok if with_ctx else dm.n_lat

    def q_map(b, g, i):
        return jnp.where(i < nl, b * nl + i, dm.n_lat // tq + b * nc + (i - nl)), g

    return pl.pallas_call(
        functools.partial(_attn_kernel, hd=hd, n_lat_tiles=nl, lat_keys=key_tiling(dm.lk),
                          ctx_keys=key_tiling(dm.ctx) if with_ctx else None),
        grid=(dm.b, dm.nkv, nl + (nc if with_ctx else 0)),
        in_specs=[
            pl.BlockSpec((tq, gw), q_map),
            pl.BlockSpec((None, None, hd, dm.lk), lambda b, g, i: (b, g, 0, 0)),
            pl.BlockSpec((None, None, dm.lk, hd), lambda b, g, i: (b, g, 0, 0)),
        ],
        out_specs=pl.BlockSpec((tq, gw), q_map),
        out_shape=jax.ShapeDtypeStruct((rows, mw), BF16),
        scratch_shapes=[
            pltpu.VMEM((GQA_GROUP * tq, 128), F32),
            pltpu.VMEM((GQA_GROUP * tq, 128), F32),
            pltpu.VMEM((GQA_GROUP * tq, hd), F32),
            pltpu.VMEM((8, 128), F32),
        ],
        compiler_params=_cparams("parallel", "parallel", "arbitrary"),
        name="attention",
    )(qp, kt, v)


def _sgu_kernel(u_ref, v_ref, g_ref, b_ref, w_ref, bs_ref, o_ref, *, ng, gw, chunk):
    ts = u_ref.shape[0]
    u = _gelu_tanh(u_ref[...].astype(F32))
    v = _ln(_gelu_tanh(v_ref[...].astype(F32))) * g_ref[...] + b_ref[...]
    vb = v.astype(BF16)
    for c in range(ts // chunk):
        rows = slice(c * chunk, (c + 1) * chunk)
        for g in range(ng):
            cols = slice(g * gw, (g + 1) * gw)
            z = jnp.dot(w_ref[g], vb[rows, cols], preferred_element_type=F32) + bs_ref[g]
            o_ref[rows, cols] = (u[rows, cols] * z).astype(o_ref.dtype)


def _sgu(proj, ln_g, ln_b, w_s, b_s, dm, rows):
    mw, chunk, ng = dm.mw, dm.sgu_chunk, dm.sgu_groups
    ts = _largest_divisor(math.gcd(dm.l, dm.n_ctx), (256, 128))
    return pl.pallas_call(
        functools.partial(_sgu_kernel, ng=ng, gw=mw // ng, chunk=chunk),
        grid=(rows // ts,),
        in_specs=[
            pl.BlockSpec((ts, mw), lambda i: (i, dm.c_u // mw)),
            pl.BlockSpec((ts, mw), lambda i: (i, dm.c_v // mw)),
            pl.BlockSpec((1, mw), lambda i: (0, 0)),
            pl.BlockSpec((1, mw), lambda i: (0, 0)),
            pl.BlockSpec((ng, chunk, chunk), lambda i: (0, 0, 0)),
            pl.BlockSpec((ng, chunk, 1), lambda i: (0, 0, 0)),
        ],
        out_specs=pl.BlockSpec((ts, mw), lambda i: (i, 0)),
        out_shape=jax.ShapeDtypeStruct((rows, mw), BF16),
        compiler_params=_cparams("parallel"),
        name="sgu",
    )(proj, proj, ln_g.reshape(1, mw), ln_b.reshape(1, mw), w_s.astype(BF16), b_s.reshape(ng, chunk, 1))


def _ret_tables(decay_logit, dm, backward):
    c, hd, nh = dm.ret_chunk, dm.hd, dm.ret_heads
    lg = jax.nn.log_sigmoid(decay_logit.astype(F32))
    pos = jnp.arange(c, dtype=F32)
    diff = pos[:, None] - pos[None, :]
    if backward:
        diff = -diff
        xi_e = c - pos
        zeta_e = pos
    else:
        xi_e = pos + 1.0
        zeta_e = c - 1.0 - pos
    dmat = jnp.where(diff >= 0, jnp.exp(jnp.maximum(diff, 0.0)[None] * lg[:, None, None]), 0.0)
    dmat = dmat * hd ** -0.5
    xi = jnp.exp(xi_e[None, :] * lg[:, None])[..., None]
    zeta = (jnp.exp(zeta_e[None, :] * lg[:, None]) * hd ** -0.5)[..., None]
    dec = jnp.broadcast_to(jnp.exp(c * lg)[:, None, None], (nh, 1, hd))
    return dmat.astype(F32), xi.astype(F32), zeta.astype(F32), dec.astype(F32)


def _ret_kernel(*refs, nh, hd, final):
    if final:
        q_ref, k_ref, v_ref, dm_ref, xi_ref, zeta_ref, dec_ref, of_ref, g_ref, o_ref, st_ref = refs
    else:
        q_ref, k_ref, v_ref, dm_ref, xi_ref, zeta_ref, dec_ref, o_ref, st_ref = refs

    @pl.when(pl.program_id(1) == 0)
    def _():
        st_ref[...] = jnp.zeros(st_ref.shape, F32)

    for h in range(nh):
        sl = slice(h * hd, (h + 1) * hd)
        qh, kh, vh = q_ref[:, sl], k_ref[:, sl], v_ref[:, sl]
        s = lax.dot_general(qh, kh, (((1,), (1,)), ((), ())), preferred_element_type=F32)
        inner = (s * dm_ref[h]).astype(BF16)
        st = st_ref[h]
        o = jnp.dot(inner, vh, preferred_element_type=F32)
        o = o + jnp.dot(qh, st.astype(BF16), preferred_element_type=F32) * xi_ref[h]
        kz = (kh.astype(F32) * zeta_ref[h]).astype(BF16)
        st_ref[h] = st * dec_ref[h] + lax.dot_general(kz, vh, (((0,), (0,)), ((), ())),
                                                      preferred_element_type=F32)
        if final:
            gn = _ln(of_ref[:, sl] + o)
            o_ref[:, sl] = (_silu(g_ref[:, sl].astype(F32)) * gn).astype(o_ref.dtype)
        else:
            o_ref[:, sl] = o


def _retention_pass(proj, tables, dm, *, backward, o_fwd=None):
    c, mw, nh, hd = dm.ret_chunk, dm.mw, dm.ret_heads, dm.hd
    n_cc, n_lc = dm.ctx // c, dm.l // c
    steps = n_cc + n_lc

    def row_block(b, t):
        if backward:
            ctx = dm.n_lat // c + b * n_cc + (n_cc - 1 - t)
            lat = b * n_lc + (n_lc - 1 - (t - n_cc))
        else:
            ctx = dm.n_lat // c + b * n_cc + t
            lat = b * n_lc + (t - n_cc)
        return jnp.where(t < n_cc, ctx, lat)

    col = lambda off: (lambda b, t: (row_block(b, t), off // mw))
    const3 = lambda b, t: (0, 0, 0)
    in_specs = [
        pl.BlockSpec((c, mw), col(dm.c_rq)),
        pl.BlockSpec((c, mw), col(dm.c_rk)),
        pl.BlockSpec((c, mw), col(dm.c_rv)),
        pl.BlockSpec((nh, c, c), const3),
        pl.BlockSpec((nh, c, 1), const3),
        pl.BlockSpec((nh, c, 1), const3),
        pl.BlockSpec((nh, 1, hd), const3),
    ]
    args = [proj, proj, proj, *tables]
    final = o_fwd is not None
    if final:
        in_specs += [pl.BlockSpec((c, mw), col(0)), pl.BlockSpec((c, mw), col(dm.c_rg))]
        args += [o_fwd, proj]
    return pl.pallas_call(
        functools.partial(_ret_kernel, nh=nh, hd=hd, final=final),
        grid=(dm.b, steps),
        in_specs=in_specs,
        out_specs=pl.BlockSpec((c, mw), col(0)),
        out_shape=jax.ShapeDtypeStruct((dm.n_tok, mw), BF16 if final else F32),
        scratch_shapes=[pltpu.VMEM((nh, hd, hd), F32)],
        compiler_params=_cparams("parallel", "arbitrary"),
        name="retention_bwd" if final else "retention_fwd",
    )(*args)


def _merge_kernel(att_ref, sgu_ref, ret_ref, ga_ref, gs_ref, gr_ref, wa_ref, ws_ref, wr_ref, wo_ref,
                  x_ref, g1_ref, lng_ref, lnb_ref, o_ref, *, alpha):
    def branch(a_ref, w_ref, gate_ref):
        return jax.nn.sigmoid(gate_ref[...].astype(F32)) * jnp.dot(a_ref[...], w_ref[...],
                                                                    preferred_element_type=F32)

    m = branch(att_ref, wa_ref, ga_ref) + branch(sgu_ref, ws_ref, gs_ref) + branch(ret_ref, wr_ref, gr_ref)
    mix = jnp.dot(m.astype(BF16), wo_ref[...], preferred_element_type=F32)
    y = alpha * x_ref[...] + g1_ref[...] * mix
    o_ref[...] = _ln(y) * lng_ref[...] + lnb_ref[...]


def _merge(att, sgu, ret, proj, w_att, w_sgu, w_ret, w_o, xs, mods, ln_g, ln_b, dm, layer, rows):
    d, mw = dm.d, dm.mw
    tm = _largest_divisor(math.gcd(dm.l, dm.n_ctx), (256, 128))
    row = dm.mod_row
    tok = lambda w: pl.BlockSpec((tm, w), lambda i: (i, 0))
    gate = lambda k: pl.BlockSpec((tm, d), lambda i: (i, dm.c_bg // d + k))
    vec = pl.BlockSpec((1, d), lambda i: (0, 0))
    return pl.pallas_call(
        functools.partial(_merge_kernel, alpha=dm.alpha),
        grid=(rows // tm,),
        in_specs=[
            tok(mw), tok(mw), tok(mw), gate(0), gate(1), gate(2),
            _resident((mw, d), lambda i: (0, 0)), _resident((mw, d), lambda i: (0, 0)),
            _resident((mw, d), lambda i: (0, 0)), _resident((d, d), lambda i: (0, 0)),
            tok(d),
            pl.BlockSpec((None, 1, d), lambda i: (row(layer, 2)(i, tm), 0, 0)),
            vec, vec,
        ],
        out_specs=tok(d),
        out_shape=jax.ShapeDtypeStruct((rows, d), F32),
        compiler_params=_cparams("parallel"),
        name="merge",
    )(att, sgu, ret, proj, proj, proj, w_att, w_sgu, w_ret, w_o, xs, mods,
      ln_g.reshape(1, d), ln_b.reshape(1, d))


def _ffn_kernel(x_ref, sh_ref, sc_ref, w1_ref, w3_ref, w2_ref, g2_ref, lng_ref, lnb_ref, o_ref,
                h_ref, acc_ref, *, alpha):
    j = pl.program_id(1)

    @pl.when(j == 0)
    def _():
        h = _ln(x_ref[...]) * (1.0 + sc_ref[...]) + sh_ref[...]
        h_ref[...] = h.astype(BF16)
        acc_ref[...] = jnp.zeros(acc_ref.shape, F32)

    h = h_ref[...]
    a = jnp.dot(h, w1_ref[...], preferred_element_type=F32)
    b = jnp.dot(h, w3_ref[...], preferred_element_type=F32)
    acc_ref[...] += jnp.dot((_silu(a) * b).astype(BF16), w2_ref[...], preferred_element_type=F32)

    @pl.when(j == pl.num_programs(1) - 1)
    def _():
        y = alpha * x_ref[...] + g2_ref[...] * acc_ref[...]
        o_ref[...] = _ln(y) * lng_ref[...] + lnb_ref[...]


def _ffn(xs, mods, w1, w3, w2, ln_g, ln_b, dm, layer):
    t, d = xs.shape
    f = w1.shape[1]
    tm = _largest_divisor(math.gcd(dm.l, dm.n_ctx), (512, 256, 128))
    tf = _largest_divisor(f, (512, 256, 128))
    row = dm.mod_row
    mod = lambda k: pl.BlockSpec((None, 1, d), lambda i, j: (row(layer, k)(i, tm), 0, 0))
    vec = pl.BlockSpec((1, d), lambda i, j: (0, 0))
    return pl.pallas_call(
        functools.partial(_ffn_kernel, alpha=dm.alpha),
        grid=(t // tm, f // tf),
        in_specs=[
            pl.BlockSpec((tm, d), lambda i, j: (i, 0)),
            mod(3), mod(4),
            pl.BlockSpec((d, tf), lambda i, j: (0, j)),
            pl.BlockSpec((d, tf), lambda i, j: (0, j)),
            pl.BlockSpec((tf, d), lambda i, j: (j, 0)),
            mod(5), vec, vec,
        ],
        out_specs=pl.BlockSpec((tm, d), lambda i, j: (i, 0)),
        out_shape=jax.ShapeDtypeStruct((t, d), F32),
        scratch_shapes=[pltpu.VMEM((tm, d), BF16), pltpu.VMEM((tm, d), F32)],
        compiler_params=_cparams("parallel", "arbitrary"),
        name="ffn_dense",
    )(xs, mods, mods, w1, w3, w2, mods, ln_g.reshape(1, d), ln_b.reshape(1, d))


def _router_kernel(x_ref, sh_ref, sc_ref, whi_ref, wlo_ref, tril_ref, h_ref, gates_ref, code_ref, rank_ref,
                   count_ref):
    h = _ln(x_ref[...]) * (1.0 + sc_ref[...]) + sh_ref[...]
    _store_token_major(h_ref, h)
    h_hi = h.astype(BF16)
    h_lo = (h - h_hi.astype(F32)).astype(BF16)
    w_hi = whi_ref[...]
    logits = (jnp.dot(h_hi, w_hi, preferred_element_type=F32)
              + jnp.dot(h_lo, w_hi, preferred_element_type=F32)
              + jnp.dot(h_hi, wlo_ref[...], preferred_element_type=F32))
    n_exp = float(logits.shape[-1])
    ids = lax.broadcasted_iota(jnp.int32, logits.shape, 1).astype(F32)
    m1 = jnp.max(logits, axis=-1, keepdims=True)
    i1 = jnp.min(jnp.where(logits == m1, ids, n_exp), axis=-1, keepdims=True)
    rest = jnp.where(ids == i1, -jnp.inf, logits)
    m2 = jnp.max(rest, axis=-1, keepdims=True)
    i2 = jnp.min(jnp.where(rest == m2, ids, n_exp), axis=-1, keepdims=True)
    e = jnp.exp(m2 - m1)
    w1 = 1.0 / (1.0 + e)
    w2 = e * w1
    gates_ref[...] = jnp.where(ids == i1, w1, jnp.where(ids == i2, w2, 0.0))
    code_ref[...] = jnp.where(ids == i1, 1, jnp.where(ids == i2, 2, 0)).astype(jnp.int32)

    @pl.when(pl.program_id(0) == 0)
    def _():
        count_ref[...] = jnp.zeros(count_ref.shape, F32)

    chosen = jnp.where(jnp.logical_or(ids == i1, ids == i2), 1.0, 0.0)
    before = jnp.dot(tril_ref[...], chosen.astype(BF16), preferred_element_type=F32) + count_ref[...]
    rank_ref[...] = before.astype(jnp.int32)
    count_ref[...] = count_ref[...] + jnp.sum(chosen, axis=0, keepdims=True)


def _router(xs, mods, router_w, dm, layer, rows):
    d, n_exp = router_w.shape
    tm = _largest_divisor(dm.l, (512, 256, 128))
    row = dm.mod_row
    mod = lambda k: pl.BlockSpec((None, 1, d), lambda i: (row(layer, k)(i, tm), 0, 0))
    w_hi = router_w.astype(BF16)
    w_lo = (router_w - w_hi.astype(F32)).astype(BF16)
    return pl.pallas_call(
        _router_kernel,
        grid=(rows // tm,),
        in_specs=[
            pl.BlockSpec((tm, d), lambda i: (i, 0)), mod(3), mod(4),
            pl.BlockSpec((d, n_exp), lambda i: (0, 0)),
            pl.BlockSpec((d, n_exp), lambda i: (0, 0)),
            pl.BlockSpec((tm, tm), lambda i: (0, 0)),
        ],
        out_specs=[
            pl.BlockSpec((tm * (d // LANES), LANES), lambda i: (i, 0)),
            pl.BlockSpec((tm, n_exp), lambda i: (i, 0)),
            pl.BlockSpec((tm, n_exp), lambda i: (i, 0)),
            pl.BlockSpec((tm, n_exp), lambda i: (i, 0)),
        ],
        out_shape=[
            jax.ShapeDtypeStruct((rows * (d // LANES), LANES), F32),
            jax.ShapeDtypeStruct((rows, n_exp), F32),
            jax.ShapeDtypeStruct((rows, n_exp), jnp.int32),
            jax.ShapeDtypeStruct((rows, n_exp), jnp.int32),
        ],
        scratch_shapes=[pltpu.VMEM((1, n_exp), F32)],
        compiler_params=_cparams("arbitrary"),
        name="router",
    )(xs, mods, mods, w_hi, w_lo, jnp.tril(jnp.ones((tm, tm), BF16), -1))


def _token_rows(ref, t, g):
    return ref.at[pl.ds(pl.multiple_of(t * g, g), g), :]


def _row_gather(index_of, lo, hi, hbm_ref, buf_ref, sem, g, *, wait):
    def body(q, carry):
        for u in range(ROW_DMA_UNROLL):
            r = lo + q * ROW_DMA_UNROLL + u
            cp = pltpu.make_async_copy(_token_rows(hbm_ref, index_of(r), g), _token_rows(buf_ref, r, g), sem)
            if wait:
                cp.wait()
            else:
                cp.start(priority=0)
        return carry

    lax.fori_loop(0, (hi - lo) // ROW_DMA_UNROLL, body, 0)


def _row_scatter(index_of, lo, hi, buf_ref, hbm_ref, sem, g, *, wait):
    def body(r, carry):
        cp = pltpu.make_async_copy(_token_rows(buf_ref, r, g), _token_rows(hbm_ref, index_of(r), g), sem)
        if wait:
            cp.wait()
        else:
            cp.start(priority=1)
        return carry

    lax.fori_loop(lo, hi, body, 0)


def _expert_kernel(te_ref, tv_ref, tok_cur, tok_nxt, dst_prv, h_hbm, w1_ref, w3_ref, w2_ref, y_hbm,
                   in_buf, out_buf, in_sem, out_sem, h_ref, acc_ref, *, nj):
    i, j = pl.program_id(0), pl.program_id(1)
    nt = pl.num_programs(0)
    tm = h_ref.shape[0]
    g = h_ref.shape[1] // LANES
    slot = i % 2
    chunk = -(-tm // (nj * ROW_DMA_UNROLL)) * ROW_DMA_UNROLL
    lo, hi = jnp.minimum(j * chunk, tm), jnp.minimum((j + 1) * chunk, tm)
    valid = tv_ref[i] > 0
    prev_rows = jnp.where(i > 0, tv_ref[jnp.maximum(i - 1, 0)], 0)
    prev_valid = prev_rows > 0
    next_valid = jnp.logical_and(i + 1 < nt, tv_ref[jnp.minimum(i + 1, nt - 1)] > 0)
    gather = functools.partial(_row_gather, hbm_ref=h_hbm, g=g)
    scatter = functools.partial(_row_scatter, lambda r: dst_prv[0, 0, r], buf_ref=out_buf.at[1 - slot],
                                hbm_ref=y_hbm, sem=out_sem.at[1 - slot], g=g)

    @pl.when(jnp.logical_and(valid, j == 0))
    def _():
        @pl.when(i == 0)
        def _():
            gather(lambda r: tok_cur[0, 0, r], 0, tm, buf_ref=in_buf.at[0], sem=in_sem.at[0], wait=False)

        gather(lambda r: tok_cur[0, 0, r], 0, tm, buf_ref=in_buf.at[slot], sem=in_sem.at[slot], wait=True)
        h_ref[...] = _load_token_major(in_buf.at[slot], tm, g).astype(BF16)
        acc_ref[...] = jnp.zeros(acc_ref.shape, F32)

    @pl.when(next_valid)
    def _():
        gather(lambda r: tok_nxt[0, 0, r], lo, hi, buf_ref=in_buf.at[1 - slot], sem=in_sem.at[1 - slot], wait=False)

    @pl.when(prev_valid)
    def _():
        scatter(jnp.minimum(lo, prev_rows), jnp.minimum(hi, prev_rows), wait=False)

    @pl.when(valid)
    def _():
        h = h_ref[...]
        a = jnp.dot(h, w1_ref[...], preferred_element_type=F32)
        b = jnp.dot(h, w3_ref[...], preferred_element_type=F32)
        acc_ref[...] += jnp.dot((_silu(a) * b).astype(BF16), w2_ref[...], preferred_element_type=F32)

        @pl.when(j == nj - 1)
        def _():
            _store_token_major(out_buf.at[slot], acc_ref[...])

    @pl.when(jnp.logical_and(prev_valid, j == nj - 1))
    def _():
        scatter(0, prev_rows, wait=True)


def _expert_mlp(h_tm, slot_token, slot_dest, tile_expert, tile_valid, w1, w3, w2, tm, d, n_out):
    n_slots = slot_token.shape[0]
    f = w1.shape[2]
    g = d // LANES
    tf = _largest_divisor(f, (1024, 512, 256, 128))
    nj, nt = f // tf, n_slots // tm
    jj = lambda i, j, tv: jnp.where(tv[i] > 0, j, nj - 1)
    tok = slot_token.reshape(nt, 1, tm)
    dst = slot_dest.reshape(nt, 1, tm)
    smem = lambda f_i: pl.BlockSpec((1, 1, tm), lambda i, j, te, tv: (f_i(i), 0, 0), memory_space=pltpu.SMEM)
    grid_spec = pltpu.PrefetchScalarGridSpec(
        num_scalar_prefetch=2,
        grid=(nt, nj),
        in_specs=[
            smem(lambda i: i),
            smem(lambda i: jnp.minimum(i + 1, nt - 1)),
            smem(lambda i: jnp.maximum(i - 1, 0)),
            pl.BlockSpec(memory_space=pl.ANY),
            pl.BlockSpec((None, d, tf), lambda i, j, te, tv: (te[i], 0, jj(i, j, tv))),
            pl.BlockSpec((None, d, tf), lambda i, j, te, tv: (te[i], 0, jj(i, j, tv))),
            pl.BlockSpec((None, tf, d), lambda i, j, te, tv: (te[i], jj(i, j, tv), 0)),
        ],
        out_specs=pl.BlockSpec(memory_space=pl.ANY),
        scratch_shapes=[
            pltpu.VMEM((2, tm * g, LANES), F32),
            pltpu.VMEM((2, tm * g, LANES), F32),
            pltpu.SemaphoreType.DMA((2,)),
            pltpu.SemaphoreType.DMA((2,)),
            pltpu.VMEM((tm, d), BF16),
            pltpu.VMEM((tm, d), F32),
        ],
    )
    return pl.pallas_call(
        functools.partial(_expert_kernel, nj=nj),
        grid_spec=grid_spec,
        out_shape=jax.ShapeDtypeStruct((n_out * g, LANES), F32),
        compiler_params=_cparams("arbitrary", "arbitrary"),
        name="expert_mlp",
    )(tile_expert, tile_valid, tok, tok, dst, h_tm, w1, w3, w2)


def _moe_out_kernel(y1_ref, y2_ref, gates_ref, code_ref, x_ref, g2_ref, lng_ref, lnb_ref, o_ref, *, alpha):
    tm, d = x_ref.shape
    g = d // LANES
    gates, code = gates_ref[...], code_ref[...]
    f = jnp.zeros((tm, d), F32)
    for k, y_ref in enumerate((y1_ref, y2_ref)):
        w = jnp.sum(jnp.where(code == k + 1, gates, 0.0), axis=-1, keepdims=True)
        f = f + w * _load_token_major(y_ref, tm, g)
    y = alpha * x_ref[...] + g2_ref[...] * f
    o_ref[...] = _ln(y) * lng_ref[...] + lnb_ref[...]


def _moe_out(y_tm, gates, code, xs, mods, ln_g, ln_b, dm, layer, rows):
    d, n_exp = dm.d, gates.shape[1]
    g = d // LANES
    tm = _largest_divisor(dm.l, (256, 128))
    nt = rows // tm
    row = dm.mod_row
    vec = pl.BlockSpec((1, d), lambda i: (0, 0))
    return pl.pallas_call(
        functools.partial(_moe_out_kernel, alpha=dm.alpha),
        grid=(nt,),
        in_specs=[
            pl.BlockSpec((tm * g, LANES), lambda i: (i, 0)),
            pl.BlockSpec((tm * g, LANES), lambda i: (nt + i, 0)),
            pl.BlockSpec((tm, n_exp), lambda i: (i, 0)),
            pl.BlockSpec((tm, n_exp), lambda i: (i, 0)),
            pl.BlockSpec((tm, d), lambda i: (i, 0)),
            pl.BlockSpec((None, 1, d), lambda i: (row(layer, 5)(i, tm), 0, 0)),
            vec, vec,
        ],
        out_specs=pl.BlockSpec((tm, d), lambda i: (i, 0)),
        out_shape=jax.ShapeDtypeStruct((rows, d), F32),
        compiler_params=_cparams("parallel"),
        name="moe_out",
    )(y_tm, y_tm, gates, code, xs, mods, ln_g.reshape(1, d), ln_b.reshape(1, d))


def _moe(xs, mods, router_w, w1, w3, w2, ln_g, ln_b, dm, layer, rows):
    n_exp = router_w.shape[1]
    tm = 512 if rows * TOP_K >= 512 * n_exp * 4 else 128
    h, gates, code, rank = _router(xs, mods, router_w, dm, layer, rows)

    counts = jnp.sum((code > 0).astype(jnp.int32), axis=0)
    padded = ((counts + tm - 1) // tm) * tm
    ends = jnp.cumsum(padded)
    starts = ends - padded
    slot_of = starts[None, :] + rank
    slot = jnp.concatenate([jnp.sum(jnp.where(code == k + 1, slot_of, 0), axis=1) for k in range(TOP_K)])
    n_slots = TOP_K * rows + (n_exp + 1) * tm
    assignment = jnp.arange(TOP_K * rows, dtype=jnp.int32)
    slot_dest = jnp.full((n_slots,), -1, jnp.int32).at[slot].set(assignment)
    slot_token = jnp.maximum(slot_dest, 0) % rows
    tile_start = jnp.arange(n_slots // tm, dtype=jnp.int32) * tm
    last_used = jnp.minimum(tile_start, ends[-1] - 1)
    tile_expert = jnp.minimum(jnp.sum((ends[None, :] <= last_used[:, None]).astype(jnp.int32), axis=1),
                              n_exp - 1).astype(jnp.int32)
    tile_rows = jnp.where(tile_start < ends[-1],
                          jnp.clip((starts + counts)[tile_expert] - tile_start, 0, tm), 0).astype(jnp.int32)

    y = _expert_mlp(h, slot_token, slot_dest, tile_expert, tile_rows, w1, w3, w2, tm, dm.d, TOP_K * rows)
    return _moe_out(y, gates, code, xs, mods, ln_g, ln_b, dm, layer, rows)


def kernel(x, c, ctx, c_ctx, ada_w, ada_b, w_in, q_norm_g, k_norm_g, sgu_ln_g, sgu_ln_b, sgu_w, sgu_b,
           ret_decay_fwd, ret_decay_bwd, w_br_att, w_br_sgu, w_br_ret, w_out, ln1_g, ln1_b, ln2_g, ln2_b,
           ffn_w1, ffn_w3, ffn_w2, router_w, moe_w1, moe_w3, moe_w2):
    dm = _Dims(x, ctx, ada_w, q_norm_g, sgu_w, ret_decay_fwd, ffn_w1, moe_w1)
    assert dm.b + 1 <= MOD_ROWS and dm.ret_heads == dm.nq
    d = dm.d

    cond = jnp.concatenate([c, c_ctx[None], jnp.zeros((MOD_ROWS - dm.b - 1, d), F32)], axis=0)
    mods = _ada_mods(cond, ada_w, ada_b)
    rope = _rope_tables(dm)

    xs = jnp.concatenate([x.reshape(dm.n_lat, d), ctx.reshape(dm.n_ctx, d)], axis=0)
    for i in range(dm.depth):
        last = i == dm.depth - 1
        rows = dm.n_lat if last else dm.n_tok

        proj = _in_projection(xs, mods, _reorder_w_in(w_in[i], dm), dm, i)

        qp, kt, v_all = _attention_prep(proj, q_norm_g[i], k_norm_g[i], rope, dm)
        att = _attention(qp, kt, v_all, dm, with_ctx=not last)

        sgu = _sgu(proj, sgu_ln_g[i], sgu_ln_b[i], sgu_w[i], sgu_b[i], dm, rows)

        o_fwd = _retention_pass(proj, _ret_tables(ret_decay_fwd[i], dm, False), dm, backward=False)
        ret = _retention_pass(proj, _ret_tables(ret_decay_bwd[i], dm, True), dm, backward=True, o_fwd=o_fwd)

        xs1 = _merge(att, sgu, ret, proj, w_br_att[i].astype(BF16), w_br_sgu[i].astype(BF16),
                     w_br_ret[i].astype(BF16), w_out[i].astype(BF16), xs, mods, ln1_g[i], ln1_b[i], dm, i, rows)

        j = i // 2
        if i % 2 == 0:
            assert rows == xs1.shape[0]
            xs = _ffn(xs1, mods, ffn_w1[j].astype(BF16), ffn_w3[j].astype(BF16), ffn_w2[j].astype(BF16),
                      ln2_g[i], ln2_b[i], dm, i)
        else:
            xs = _moe(xs1, mods, router_w[j], moe_w1[j].astype(BF16), moe_w3[j].astype(BF16),
                      moe_w2[j].astype(BF16), ln2_g[i], ln2_b[i], dm, i, rows)
    return xs[:dm.n_lat].reshape(dm.b, dm.l, d)
```
